```python
import math
import jax, jax.numpy as jnp
from jax import lax
import numpy as np

D_MODEL = 1024
BATCH = 4
SEQ = 4096
DEPTH = 2

ATTN_HEADS = 8
HEAD_DIM = 64
ATTN_WIDTH = ATTN_HEADS * HEAD_DIM
CONV_GROUPS = 8
CONV_WIDTH = D_MODEL - ATTN_WIDTH
CONV_KERNEL = 31
MIX_IN = 3 * ATTN_WIDTH + 2 * CONV_WIDTH
ROPE_THETA = 10000.0
MOBA_BLOCK = 256
MOBA_TOPK = 3
MOBA_Q_CHUNK = 32
PEER_HEADS = 8
PEER_NKEYS = 128
PEER_EXPERTS = PEER_NKEYS * PEER_NKEYS
PEER_QDIM = 256
PEER_HALF = PEER_QDIM // 2
PEER_TOPK = 16
PEER_TOK_CHUNK = 128
DEEPNORM_ALPHA = (2 * DEPTH) ** 0.25
DEEPNORM_BETA = (8 * DEPTH) ** -0.25
LN_EPS = 1e-5

kernel_name = "hymba_moba_conformer_peer_deepnorm"


def layer_norm(x, g, b):
    xf = x.astype(jnp.float32)
    mu = jnp.mean(xf, axis=-1, keepdims=True)
    xc = xf - mu
    var = jnp.mean(xc * xc, axis=-1, keepdims=True)
    y = xc * lax.rsqrt(var + LN_EPS) * g.astype(jnp.float32) + b.astype(jnp.float32)
    return y.astype(x.dtype)


def rope_tables(S):
    pos = jnp.arange(S, dtype=jnp.float32)
    inv = jnp.power(ROPE_THETA, -jnp.arange(0, HEAD_DIM, 2, dtype=jnp.float32) / HEAD_DIM)
    ang = pos[:, None] * inv[None, :]
    return jnp.cos(ang), jnp.sin(ang)


def apply_rope(t, cos, sin):
    tf = t.astype(jnp.float32)
    half = HEAD_DIM // 2
    t1, t2 = tf[..., :half], tf[..., half:]
    out = jnp.concatenate([t1 * cos - t2 * sin, t2 * cos + t1 * sin], axis=-1)
    return out.astype(t.dtype)


def split_heads(t, B, S):
    return t.reshape(B, S, ATTN_HEADS, HEAD_DIM).transpose(0, 2, 1, 3)


def moba_attention(q, k, v):
    B, H, S, Dh = q.shape
    nb = -(-S // MOBA_BLOCK)
    sp = nb * MOBA_BLOCK
    pad = ((0, 0), (0, 0), (0, sp - S), (0, 0))
    kb = jnp.pad(k, pad).reshape(B, H, nb, MOBA_BLOCK, Dh)
    vb = jnp.pad(v, pad).reshape(B, H, nb, MOBA_BLOCK, Dh)
    k_mean = jnp.mean(kb.astype(jnp.float32), axis=3)
    gate = jnp.einsum('bhsd,bhnd->bhsn', q.astype(jnp.float32), k_mean)
    q_blk = jnp.arange(S) // MOBA_BLOCK
    fully_past = jnp.arange(nb)[None, :] < q_blk[:, None]
    gate = jnp.where(fully_past, gate, -jnp.inf)
    n_sel = max(1, min(MOBA_TOPK, nb - 1))
    sel_score, sel_idx = lax.top_k(gate, n_sel)
    sel_valid = jnp.isfinite(sel_score)

    nc = S // MOBA_Q_CHUNK
    scale = HEAD_DIM ** -0.5
    b_ix = jnp.arange(B)[:, None, None, None]
    h_ix = jnp.arange(H)[None, :, None, None]

    def to_chunks(a):
        a = a.reshape(B, H, nc, MOBA_Q_CHUNK, *a.shape[3:])
        return jnp.moveaxis(a, 2, 0)

    def chunk_fn(args):
        q_c, idx_c, valid_c, start = args
        k_sel = kb[b_ix, h_ix, idx_c]
        v_sel = vb[b_ix, h_ix, idx_c]
        own = start // MOBA_BLOCK
        k_own = lax.dynamic_index_in_dim(kb, own, axis=2, keepdims=False)
        v_own = lax.dynamic_index_in_dim(vb, own, axis=2, keepdims=False)
        s_sel = jnp.einsum('bhqd,bhqrpd->bhqrp', q_c, k_sel,
                           preferred_element_type=jnp.float32) * scale
        s_sel = jnp.where(valid_c[..., None], s_sel, -jnp.inf)
        s_sel = s_sel.reshape(B, H, MOBA_Q_CHUNK, n_sel * MOBA_BLOCK)
        q_pos = start + jnp.arange(MOBA_Q_CHUNK)
        k_pos = own * MOBA_BLOCK + jnp.arange(MOBA_BLOCK)
        s_own = jnp.einsum('bhqd,bhpd->bhqp', q_c, k_own,
                           preferred_element_type=jnp.float32) * scale
        s_own = jnp.where(k_pos[None, :] <= q_pos[:, None], s_own, -jnp.inf)
        p = jax.nn.softmax(jnp.concatenate([s_sel, s_own], axis=-1), axis=-1).astype(v.dtype)
        p_sel = p[..., :n_sel * MOBA_BLOCK].reshape(B, H, MOBA_Q_CHUNK, n_sel, MOBA_BLOCK)
        p_own = p[..., n_sel * MOBA_BLOCK:]
        return (jnp.einsum('bhqrp,bhqrpd->bhqd', p_sel, v_sel)
                + jnp.einsum('bhqp,bhpd->bhqd', p_own, v_own))

    starts = jnp.arange(nc, dtype=jnp.int32) * MOBA_Q_CHUNK
    out = lax.map(chunk_fn, (to_chunks(q), to_chunks(sel_idx), to_chunks(sel_valid), starts))
    return jnp.moveaxis(out, 0, 2).reshape(B, H, S, Dh)


def conformer_conv(a, gate, w, b, g, beta):
    u = a * jax.nn.sigmoid(gate)
    u = lax.conv_general_dilated(u, w[:, None, :], window_strides=(1,),
                                 padding=[(CONV_KERNEL - 1, 0)],
                                 dimension_numbers=('NWC', 'WIO', 'NWC'),
                                 feature_group_count=CONV_WIDTH) + b
    return jax.nn.silu(layer_norm(u, g, beta))


def peer_ffn(x, w_q, sub_keys, u_tab, v_tab):
    B, S, D = x.shape
    q = (x @ w_q).reshape(B, S, PEER_HEADS, 2, PEER_HALF)
    s = jnp.einsum('bshcd,hckd->bshck', q, sub_keys, preferred_element_type=jnp.float32)
    top_s, top_i = lax.top_k(s, PEER_TOPK)
    comb = top_s[..., 0, :, None] + top_s[..., 1, None, :]
    comb = comb.reshape(B, S, PEER_HEADS, PEER_TOPK * PEER_TOPK)
    best_s, best_c = lax.top_k(comb, PEER_TOPK)
    i1 = jnp.take_along_axis(top_i[..., 0, :], best_c // PEER_TOPK, axis=-1)
    i2 = jnp.take_along_axis(top_i[..., 1, :], best_c % PEER_TOPK, axis=-1)
    expert = (i1 * PEER_NKEYS + i2).reshape(B, S, PEER_HEADS * PEER_TOPK)
    gates = jax.nn.softmax(best_s, axis=-1).reshape(B, S, PEER_HEADS * PEER_TOPK).astype(x.dtype)

    nc = S // PEER_TOK_CHUNK

    def to_chunks(a):
        a = a.reshape(B, nc, PEER_TOK_CHUNK, *a.shape[2:])
        return jnp.moveaxis(a, 1, 0)

    def chunk_fn(args):
        x_c, e_c, g_c = args
        u = u_tab[e_c]
        h = jax.nn.gelu(jnp.einsum('bcd,bced->bce', x_c, u), approximate=False)
        return jnp.einsum('bce,bced->bcd', g_c * h, v_tab[e_c])

    y = lax.map(chunk_fn, (to_chunks(x), to_chunks(expert), to_chunks(gates)))
    return jnp.moveaxis(y, 0, 1).reshape(B, S, D)


def setup_inputs(seed: int = 0) -> dict:
    key = jax.random.key(seed)
    ks = jax.random.split(key, 16)
    f32 = jnp.float32
    x = jax.random.normal(ks[0], (BATCH, SEQ, D_MODEL), f32)
    col_scale = jnp.concatenate([
        jnp.ones((2 * ATTN_WIDTH,), f32),
        jnp.full((ATTN_WIDTH,), DEEPNORM_BETA, f32),
        jnp.ones((2 * CONV_WIDTH,), f32)])
    w_in = jax.random.normal(ks[1], (DEPTH, D_MODEL, MIX_IN), f32) * D_MODEL ** -0.5 * col_scale
    conv_w = jax.random.normal(ks[2], (DEPTH, CONV_KERNEL, CONV_WIDTH), f32) * CONV_KERNEL ** -0.5
    conv_b = 0.01 * jax.random.normal(ks[3], (DEPTH, CONV_WIDTH), f32)
    conv_ln_g = 1.0 + 0.01 * jax.random.normal(ks[4], (DEPTH, CONV_WIDTH), f32)
    conv_ln_b = 0.01 * jax.random.normal(ks[5], (DEPTH, CONV_WIDTH), f32)
    w_out = jax.random.normal(ks[6], (DEPTH, D_MODEL, D_MODEL), f32) * D_MODEL ** -0.5 * DEEPNORM_BETA
    ln1_g = 1.0 + 0.01 * jax.random.normal(ks[7], (DEPTH, D_MODEL), f32)
    ln1_b = 0.01 * jax.random.normal(ks[8], (DEPTH, D_MODEL), f32)
    peer_wq = jax.random.normal(ks[9], (DEPTH, D_MODEL, PEER_HEADS * PEER_QDIM), f32) * D_MODEL ** -0.5
    peer_keys = jax.random.normal(ks[10], (DEPTH, PEER_HEADS, 2, PEER_NKEYS, PEER_HALF), f32) * PEER_HALF ** -0.5
    peer_u = jax.random.normal(ks[11], (DEPTH, PEER_EXPERTS, D_MODEL), f32) * D_MODEL ** -0.5
    peer_v = jax.random.normal(ks[12], (DEPTH, PEER_EXPERTS, D_MODEL), f32) * DEEPNORM_BETA
    ln2_g = 1.0 + 0.01 * jax.random.normal(ks[13], (DEPTH, D_MODEL), f32)
    ln2_b = 0.01 * jax.random.normal(ks[14], (DEPTH, D_MODEL), f32)
    return {"x": x, "w_in": w_in, "conv_w": conv_w, "conv_b": conv_b,
            "conv_ln_g": conv_ln_g, "conv_ln_b": conv_ln_b, "w_out": w_out,
            "ln1_g": ln1_g, "ln1_b": ln1_b, "peer_wq": peer_wq, "peer_keys": peer_keys,
            "peer_u": peer_u, "peer_v": peer_v, "ln2_g": ln2_g, "ln2_b": ln2_b}


def reference(x, w_in, conv_w, conv_b, conv_ln_g, conv_ln_b, w_out, ln1_g, ln1_b,
              peer_wq, peer_keys, peer_u, peer_v, ln2_g, ln2_b):
    B, S, _ = x.shape
    cos, sin = rope_tables(S)
    cuts = [ATTN_WIDTH, 2 * ATTN_WIDTH, 3 * ATTN_WIDTH, 3 * ATTN_WIDTH + CONV_WIDTH]
    for l in range(DEPTH):
        proj = x @ w_in[l]
        q, k, v, a, g = jnp.split(proj, cuts, axis=-1)
        q = apply_rope(split_heads(q, B, S), cos, sin)
        k = apply_rope(split_heads(k, B, S), cos, sin)
        attn = moba_attention(q, k, split_heads(v, B, S))
        attn = attn.transpose(0, 2, 1, 3).reshape(B, S, ATTN_WIDTH)
        conv = conformer_conv(a, g, conv_w[l], conv_b[l], conv_ln_g[l], conv_ln_b[l])
        mixed = jnp.concatenate([attn, conv], axis=-1) @ w_out[l]
        x = layer_norm(DEEPNORM_ALPHA * x + mixed, ln1_g[l], ln1_b[l])
        ffn = peer_ffn(x, peer_wq[l], peer_keys[l], peer_u[l], peer_v[l])
        x = layer_norm(DEEPNORM_ALPHA * x + ffn, ln2_g[l], ln2_b[l])
    return x
```

```python
import functools
import math

import jax
import jax.numpy as jnp
from jax import lax
from jax.experimental import pallas as pl
from jax.experimental.pallas import tpu as pltpu

ATTN_HEADS = 8
HEAD_DIM = 64
ATTN_WIDTH = ATTN_HEADS * HEAD_DIM
CONV_KERNEL = 31
ROPE_THETA = 10000.0
MOBA_BLOCK = 256
MOBA_TOPK = 3
PEER_HEADS = 8
PEER_NKEYS = 128
PEER_HALF = 128
PEER_TOPK = 16
LN_EPS = 1e-5

LANES = 128
SUBLANES = 8
VMEM_LIMIT_BYTES = 56 * 1024 * 1024

NEG_BIG = -1e30
CONV_HALO = 32

F32 = jnp.float32
BF16 = jnp.bfloat16


def _params(*semantics):
    return pltpu.CompilerParams(dimension_semantics=semantics,
                                vmem_limit_bytes=VMEM_LIMIT_BYTES)


def _layer_norm_rows(y, g, b):
    mu = jnp.mean(y, axis=-1, keepdims=True)
    yc = y - mu
    var = jnp.mean(yc * yc, axis=-1, keepdims=True)
    return yc * lax.rsqrt(var + LN_EPS) * g + b


def _matmul_kernel(a_ref, b_ref, o_ref):
    a = a_ref[...].astype(BF16)
    o_ref[...] = jnp.dot(a, b_ref[...], preferred_element_type=F32).astype(o_ref.dtype)


def _matmul(a, b, *, tm, tn, out_dtype, name):
    m, k = a.shape
    _, n = b.shape
    return pl.pallas_call(
        _matmul_kernel,
        grid=(m // tm, n // tn),
        in_specs=[pl.BlockSpec((tm, k), lambda i, j: (i, 0)),
                  pl.BlockSpec((k, tn), lambda i, j: (0, j))],
        out_specs=pl.BlockSpec((tm, tn), lambda i, j: (i, j)),
        out_shape=jax.ShapeDtypeStruct((m, n), out_dtype),
        compiler_params=_params("parallel", "parallel"),
        name=name,
    )(a, b)


def _rope_kernel(q_ref, qr_ref, k_ref, kr_ref, v_ref, cos_ref, sin_ref,
                 qo_ref, ko_ref, vo_ref, km_ref):
    cos = cos_ref[...]
    sin = sin_ref[...]
    qo_ref[...] = q_ref[...] * cos + qr_ref[...] * sin
    k = k_ref[...] * cos + kr_ref[...] * sin
    ko_ref[...] = k.astype(BF16)
    vo_ref[...] = v_ref[...].astype(BF16)
    km_ref[0] = jnp.mean(k, axis=0, keepdims=True)


def _rope(proj, cos, sin, seq):
    t = proj.shape[0]
    w = ATTN_WIDTH
    nblk = t // MOBA_BLOCK
    per_seq = seq // MOBA_BLOCK
    col = lambda c: pl.BlockSpec((MOBA_BLOCK, w), lambda i, c=c: (i, c))
    tab = pl.BlockSpec((MOBA_BLOCK, w), lambda i: (i % per_seq, 0))
    row = pl.BlockSpec((MOBA_BLOCK, w), lambda i: (i, 0))
    return pl.pallas_call(
        _rope_kernel,
        grid=(nblk,),
        in_specs=[col(0), col(1), col(2), col(3), col(4), tab, tab],
        out_specs=[row, row, row, pl.BlockSpec((1, 1, w), lambda i: (i, 0, 0))],
        out_shape=[jax.ShapeDtypeStruct((t, w), F32),
                   jax.ShapeDtypeStruct((t, w), BF16),
                   jax.ShapeDtypeStruct((t, w), BF16),
                   jax.ShapeDtypeStruct((nblk, 1, w), F32)],
        compiler_params=_params("parallel"),
        name="rope_kmean",
    )(proj, proj, proj, proj, proj, cos, sin)


def _moba_kernel(q_ref, k_ref, v_ref, km_ref, e_ref, o_ref, m_scr, l_scr, acc_scr,
                 *, scale, n_sel):
    blk = MOBA_BLOCK
    i = pl.program_id(2)
    q = q_ref[...]
    lane = lax.broadcasted_iota(jnp.int32, (blk, LANES), 1)
    km = km_ref[0]
    qexts = []
    for h in range(2):
        in_head = (lane // HEAD_DIM) == h
        qh = jnp.where(in_head, q, 0.0)
        gate = lax.dot_general(qh, km, (((1,), (1,)), ((), ())),
                               precision=lax.Precision.HIGHEST,
                               preferred_element_type=F32)
        g = jnp.where(lane < i, gate, -jnp.inf)
        bias = jnp.full((blk, LANES), NEG_BIG, F32)
        for _ in range(n_sel):
            mx = jnp.max(g, axis=1, keepdims=True)
            first = jnp.min(jnp.where(g == mx, lane, LANES), axis=1, keepdims=True)
            first = jnp.where(mx > -jnp.inf, first, LANES)
            pick = lane == first
            bias = jnp.where(pick, 0.0, bias)
            g = jnp.where(pick, -jnp.inf, g)
        qexts.append(jnp.concatenate([(qh * scale).astype(BF16), bias.astype(BF16)], axis=1))
    qext = jnp.concatenate(qexts, axis=0)

    own = pl.multiple_of(i * blk, blk)
    kd = k_ref[pl.ds(own, blk), :]
    vd = v_ref[pl.ds(own, blk), :]
    s = lax.dot_general(qext[:, :LANES], kd, (((1,), (1,)), ((), ())),
                        preferred_element_type=F32)
    r_ix = lax.broadcasted_iota(jnp.int32, (2 * blk, blk), 0) % blk
    c_ix = lax.broadcasted_iota(jnp.int32, (2 * blk, blk), 1)
    s = jnp.where(c_ix <= r_ix, s, NEG_BIG)
    m0 = jnp.max(s, axis=1, keepdims=True)
    p = jnp.exp(s - m0)
    m_scr[...] = jnp.broadcast_to(m0, m_scr.shape)
    l_scr[...] = jnp.broadcast_to(jnp.sum(p, axis=1, keepdims=True), l_scr.shape)
    acc_scr[...] = jnp.dot(p.astype(BF16), vd, preferred_element_type=F32)

    def body(j, carry):
        off = pl.multiple_of(j * blk, blk)
        kext = jnp.concatenate([k_ref[pl.ds(off, blk), :], e_ref[pl.ds(off, blk), :]], axis=1)
        sj = lax.dot_general(qext, kext, (((1,), (1,)), ((), ())),
                             preferred_element_type=F32)
        m_prev = m_scr[...]
        m_new = jnp.maximum(m_prev, jnp.max(sj, axis=1, keepdims=True))
        alpha = jnp.exp(m_prev - m_new)
        pj = jnp.exp(sj - jnp.concatenate([m_new, m_new], axis=1))
        l_scr[...] = alpha * l_scr[...] + jnp.sum(pj, axis=1, keepdims=True)
        acc_scr[...] = alpha * acc_scr[...] + jnp.dot(
            pj.astype(BF16), v_ref[pl.ds(off, blk), :], preferred_element_type=F32)
        m_scr[...] = m_new
        return carry

    lax.fori_loop(0, i, body, 0)
    o = acc_scr[...] / l_scr[...]
    o_ref[...] = jnp.where(lane < HEAD_DIM, o[:blk], o[blk:]).astype(o_ref.dtype)


def _moba(qf, kb, vb, kmean, eye, batch, seq):
    t = qf.shape[0]
    blk = MOBA_BLOCK
    nb = seq // blk
    n_sel = max(1, min(MOBA_TOPK, nb - 1))
    pairs = ATTN_WIDTH // LANES
    kernel = functools.partial(_moba_kernel, scale=HEAD_DIM ** -0.5, n_sel=n_sel)
    return pl.pallas_call(
        kernel,
        grid=(batch, pairs, nb),
        in_specs=[pl.BlockSpec((blk, LANES), lambda b, p, i: (b * nb + i, p)),
                  pl.BlockSpec((seq, LANES), lambda b, p, i: (b, p)),
                  pl.BlockSpec((seq, LANES), lambda b, p, i: (b, p)),
                  pl.BlockSpec((1, LANES, LANES), lambda b, p, i: (b, 0, p)),
                  pl.BlockSpec((seq, LANES), lambda b, p, i: (0, 0))],
        out_specs=pl.BlockSpec((blk, LANES), lambda b, p, i: (b * nb + i, p)),
        out_shape=jax.ShapeDtypeStruct((t, ATTN_WIDTH), BF16),
        scratch_shapes=[pltpu.VMEM((2 * blk, LANES), F32),
                        pltpu.VMEM((2 * blk, LANES), F32),
                        pltpu.VMEM((2 * blk, LANES), F32)],
        compiler_params=_params("parallel", "parallel", "arbitrary"),
        name="moba_attention",
    )(qf, kb, vb, kmean, eye)


def _conv_kernel(a_ref, g_ref, ah_ref, gh_ref, w_ref, b_ref, lg_ref, lb_ref, o_ref, u_scr,
                 *, ts):
    j = pl.program_id(1)
    halo = ah_ref[...] * jax.nn.sigmoid(gh_ref[...])
    u_scr[0:CONV_HALO, :] = jnp.where(j > 0, halo, 0.0)
    u_scr[CONV_HALO:, :] = a_ref[...] * jax.nn.sigmoid(g_ref[...])
    first = CONV_HALO - (CONV_KERNEL - 1)
    acc = jnp.zeros((ts, a_ref.shape[1]), F32)
    for tap in range(CONV_KERNEL):
        acc = acc + w_ref[tap:tap + 1, :] * u_scr[first + tap:first + tap + ts, :]
    y = _layer_norm_rows(acc + b_ref[...], lg_ref[...], lb_ref[...])
    o_ref[...] = (y * jax.nn.sigmoid(y)).astype(o_ref.dtype)


def _conv(proj, a_col, g_col, w, b, ln_g, ln_b, batch, seq, ts):
    t = proj.shape[0]
    cw = w.shape[1]
    per_seq = seq // ts
    halo_per_tile = ts // CONV_HALO
    cur = lambda c: pl.BlockSpec((ts, cw), lambda bi, j, c=c: (bi * per_seq + j, c))
    halo = lambda c: pl.BlockSpec(
        (CONV_HALO, cw),
        lambda bi, j, c=c: (jnp.maximum((bi * per_seq + j) * halo_per_tile - 1, 0), c))
    vec = pl.BlockSpec((1, cw), lambda bi, j: (0, 0))
    return pl.pallas_call(
        functools.partial(_conv_kernel, ts=ts),
        grid=(batch, per_seq),
        in_specs=[cur(a_col), cur(g_col), halo(a_col), halo(g_col),
                  pl.BlockSpec((CONV_KERNEL, cw), lambda bi, j: (0, 0)), vec, vec, vec],
        out_specs=pl.BlockSpec((ts, cw), lambda bi, j: (bi * per_seq + j, 0)),
        out_shape=jax.ShapeDtypeStruct((t, cw), BF16),
        scratch_shapes=[pltpu.VMEM((CONV_HALO + ts, cw), F32)],
        compiler_params=_params("parallel", "parallel"),
        name="conformer_conv",
    )(proj, proj, proj, proj, w, b.reshape(1, cw), ln_g.reshape(1, cw), ln_b.reshape(1, cw))


def _outproj_kernel(at_ref, cv_ref, x_ref, w_ref, g_ref, b_ref, o_ref, ot_ref, *, alpha):
    aw = at_ref.shape[1]
    mixed = jnp.dot(at_ref[...], w_ref[0:aw, :], preferred_element_type=F32)
    mixed = mixed + jnp.dot(cv_ref[...], w_ref[aw:, :], preferred_element_type=F32)
    y = _layer_norm_rows(alpha * x_ref[...] + mixed, g_ref[...], b_ref[...])
    o_ref[...] = y
    ot_ref[...] = y.T.astype(BF16)


def _outproj(attn, conv, x, w_out, g, b, alpha, tm):
    t, d = x.shape
    aw = attn.shape[1]
    cw = conv.shape[1]
    vec = pl.BlockSpec((1, d), lambda i: (0, 0))
    return pl.pallas_call(
        functools.partial(_outproj_kernel, alpha=alpha),
        grid=(t // tm,),
        in_specs=[pl.BlockSpec((tm, aw), lambda i: (i, 0)),
                  pl.BlockSpec((tm, cw), lambda i: (i, 0)),
                  pl.BlockSpec((tm, d), lambda i: (i, 0)),
                  pl.BlockSpec((aw + cw, d), lambda i: (0, 0)), vec, vec],
        out_specs=[pl.BlockSpec((tm, d), lambda i: (i, 0)),
                   pl.BlockSpec((d, tm), lambda i: (0, i))],
        out_shape=[jax.ShapeDtypeStruct((t, d), F32),
                   jax.ShapeDtypeStruct((d, t), BF16)],
        compiler_params=_params("parallel"),
        name="outproj_ln",
    )(attn, conv, x, w_out, g.reshape(1, d), b.reshape(1, d))


def _fold_kernel(k_ref, w_ref, o_ref):
    o_ref[...] = lax.dot_general(k_ref[0], w_ref[...], (((1,), (1,)), ((), ())),
                                 precision=lax.Precision.HIGHEST,
                                 preferred_element_type=F32).astype(o_ref.dtype)


def _fold_keys(keys, wq):
    d = wq.shape[0]
    groups = keys.shape[0]
    return pl.pallas_call(
        _fold_kernel,
        grid=(groups,),
        in_specs=[pl.BlockSpec((1, PEER_NKEYS, PEER_HALF), lambda i: (i, 0, 0)),
                  pl.BlockSpec((d, PEER_HALF), lambda i: (0, i))],
        out_specs=pl.BlockSpec((PEER_NKEYS, d), lambda i: (i, 0)),
        out_shape=jax.ShapeDtypeStruct((groups * PEER_NKEYS, d), BF16),
        compiler_params=_params("parallel"),
        name="peer_fold_keys",
    )(keys, wq)


def _top_values_and_ranks(s, k):
    rank = jnp.full(s.shape, float(k), F32)
    cur = s
    tops = []
    for r in range(k):
        mx = jnp.max(cur, axis=0, keepdims=True)
        hit = cur == mx
        rank = jnp.where(hit, float(r), rank)
        cur = jnp.where(hit, -jnp.inf, cur)
        tops.append(mx)
    return jnp.concatenate(tops, axis=0), rank


def _route_kernel(s_ref, r2_ref, e2_ref, cnt_ref, e1_ref):
    nk = PEER_NKEYS
    k = PEER_TOPK

    def head(h, carry):
        base = pl.multiple_of(h * 2 * nk, 2 * nk)
        s1 = s_ref[pl.ds(base, nk), :]
        s2 = s_ref[pl.ds(base + nk, nk), :]
        a, rank1 = _top_values_and_ranks(s1, k)
        b, rank2 = _top_values_and_ranks(s2, k)
        z = jnp.concatenate([a[i:i + 1, :] + b for i in range(k)], axis=0)
        cur = z
        for _ in range(k):
            mx = jnp.max(cur, axis=0, keepdims=True)
            cur = jnp.where(cur == mx, -jnp.inf, cur)
        chosen = cur == -jnp.inf
        zmax = a[0:1, :] + b[0:1, :]
        denom = jnp.sum(jnp.where(chosen, jnp.exp(z - zmax), 0.0), axis=0, keepdims=True)
        ones = jnp.where(chosen, 1.0, 0.0)
        cnt = jnp.zeros(s1.shape, F32)
        for i in range(k):
            cnt_i = jnp.sum(ones[i * k:(i + 1) * k, :], axis=0, keepdims=True)
            cnt = jnp.where(rank1 == float(i), cnt_i, cnt)
        out = pl.multiple_of(h * nk, nk)
        r2_ref[pl.ds(out, nk), :] = rank2
        e2_ref[pl.ds(out, nk), :] = jnp.exp(s2 - b[0:1, :])
        cnt_ref[pl.ds(out, nk), :] = cnt
        e1_ref[pl.ds(out, nk), :] = jnp.exp(s1 - a[0:1, :]) / denom
        return carry

    lax.fori_loop(0, PEER_HEADS, head, 0)


def _route(scores_t, tl):
    rows, t = scores_t.shape
    out_rows = rows // 2
    spec = pl.BlockSpec((out_rows, tl), lambda i: (0, i))
    shape = jax.ShapeDtypeStruct((out_rows, t), F32)
    return pl.pallas_call(
        _route_kernel,
        grid=(t // tl,),
        in_specs=[pl.BlockSpec((rows, tl), lambda i: (0, i))],
        out_specs=[spec, spec, spec, spec],
        out_shape=[shape, shape, shape, shape],
        compiler_params=_params("parallel"),
        name="peer_route",
    )(scores_t)


def _peer_kernel(u_ref, xt_ref, vt_ref, r2_ref, e2_ref, cnt_ref, e1_ref, x_ref, g_ref, b_ref,
                 o_ref, acc_scr, a_scr, *, alpha):
    nk = PEER_NKEYS
    kk = pl.program_id(1)

    @pl.when(kk == 0)
    def _():
        acc_scr[...] = jnp.zeros_like(acc_scr)

    ht = jnp.dot(u_ref[...], xt_ref[...], preferred_element_type=F32)
    act = 0.5 * ht * (1.0 + lax.erf(ht * (2.0 ** -0.5)))
    groups = u_ref.shape[0] // nk
    for c in range(groups):
        gate = jnp.zeros((nk, ht.shape[1]), F32)
        for h in range(PEER_HEADS):
            r2 = r2_ref[h * nk:(h + 1) * nk, :]
            e2 = e2_ref[h * nk:(h + 1) * nk, :]
            gate = gate + jnp.where(r2 < cnt_ref[c, h:h + 1, :], e2, 0.0) * e1_ref[c, h:h + 1, :]
        a_scr[c * nk:(c + 1) * nk, :] = (gate * act[c * nk:(c + 1) * nk, :]).astype(BF16)
    acc_scr[...] += jnp.dot(vt_ref[...], a_scr[...], preferred_element_type=F32)

    @pl.when(kk == pl.num_programs(1) - 1)
    def _():
        y = alpha * x_ref[...] + acc_scr[...].T
        o_ref[...] = _layer_norm_rows(y, g_ref[...], b_ref[...])


def _peer(u_b, x_t, v_t, r2, e2, cnt, e1, x, g, b, alpha, tt, tn):
    n, d = u_b.shape
    t = x.shape[0]
    groups = tn // PEER_NKEYS
    tok = pl.BlockSpec((PEER_HEADS * PEER_NKEYS, tt), lambda j, k: (0, j))
    per_key = pl.BlockSpec((groups, PEER_HEADS, tt), lambda j, k: (k, 0, j))
    vec = pl.BlockSpec((1, d), lambda j, k: (0, 0))
    return pl.pallas_call(
        functools.partial(_peer_kernel, alpha=alpha),
        grid=(t // tt, n // tn),
        in_specs=[pl.BlockSpec((tn, d), lambda j, k: (k, 0)),
                  pl.BlockSpec((d, tt), lambda j, k: (0, j)),
                  pl.BlockSpec((d, tn), lambda j, k: (0, k)),
                  tok, tok, per_key, per_key,
                  pl.BlockSpec((tt, d), lambda j, k: (j, 0)), vec, vec],
        out_specs=pl.BlockSpec((tt, d), lambda j, k: (j, 0)),
        out_shape=jax.ShapeDtypeStruct((t, d), F32),
        scratch_shapes=[pltpu.VMEM((d, tt), F32), pltpu.VMEM((tn, tt), BF16)],
        compiler_params=_params("parallel", "arbitrary"),
        name="peer_experts",
    )(u_b, x_t, v_t, r2, e2, cnt, e1, x, g.reshape(1, d), b.reshape(1, d))


def _rotate_half_columns(w):
    d, n = w.shape
    wh = w.reshape(d, n // HEAD_DIM, 2, HEAD_DIM // 2)
    return jnp.stack([-wh[:, :, 1], wh[:, :, 0]], axis=2).reshape(d, n)


def _rope_tables(seq):
    pos = jnp.arange(seq, dtype=F32)
    inv = jnp.power(ROPE_THETA, -jnp.arange(0, HEAD_DIM, 2, dtype=F32) / HEAD_DIM)
    ang = pos[:, None] * inv[None, :]
    cos = jnp.tile(jnp.cos(ang), (1, 2 * ATTN_HEADS))
    sin = jnp.tile(jnp.sin(ang), (1, 2 * ATTN_HEADS))
    return cos, sin


def kernel(x, w_in, conv_w, conv_b, conv_ln_g, conv_ln_b, w_out, ln1_g, ln1_b,
           peer_wq, peer_keys, peer_u, peer_v, ln2_g, ln2_b):
    batch, seq, d = x.shape
    depth = w_in.shape[0]
    t = batch * seq
    aw = ATTN_WIDTH
    alpha = (2 * depth) ** 0.25
    nb = seq // MOBA_BLOCK

    cos, sin = _rope_tables(seq)
    eye = (jnp.arange(seq)[:, None] // MOBA_BLOCK == jnp.arange(LANES)[None, :]).astype(BF16)

    xf = x.reshape(t, d)
    for l in range(depth):
        wq, wk, rest = w_in[l][:, :aw], w_in[l][:, aw:2 * aw], w_in[l][:, 2 * aw:]
        w_ext = jnp.concatenate(
            [wq, _rotate_half_columns(wq), wk, _rotate_half_columns(wk), rest], axis=1).astype(BF16)
        proj = _matmul(xf, w_ext, tm=256, tn=w_ext.shape[1], out_dtype=F32, name="in_proj")
        qf, kb, vb, kmean = _rope(proj, cos, sin, seq)
        kmean = jnp.pad(kmean.reshape(batch, nb, aw), ((0, 0), (0, LANES - nb), (0, 0)))
        attn = _moba(qf, kb, vb, kmean, eye, batch, seq)
        conv = _conv(proj, 5, 6, conv_w[l], conv_b[l], conv_ln_g[l], conv_ln_b[l],
                     batch, seq, ts=512)
        x1, x1t = _outproj(attn, conv, xf, w_out[l].astype(BF16), ln1_g[l], ln1_b[l], alpha, tm=256)

        groups = PEER_HEADS * 2
        ws_t = _fold_keys(peer_keys[l].reshape(groups, PEER_NKEYS, PEER_HALF), peer_wq[l])
        scores_t = _matmul(ws_t, x1t, tm=ws_t.shape[0], tn=512, out_dtype=F32, name="peer_scores")
        r2, e2, cnt, e1 = _route(scores_t, tl=256)
        by_key = lambda a: a.reshape(PEER_HEADS, PEER_NKEYS, t).transpose(1, 0, 2)
        xf = _peer(peer_u[l].astype(BF16), x1t, peer_v[l].T.astype(BF16),
                   r2, e2, by_key(cnt), by_key(e1), x1, ln2_g[l], ln2_b[l], alpha, tt=512, tn=512)
    return xf.reshape(batch, seq, d)
```

```python
import functools

import jax
import jax.numpy as jnp
from jax import lax
from jax.experimental import pallas as pl
from jax.experimental.pallas import tpu as pltpu

ATTN_HEADS = 8
HEAD_DIM = 64
ATTN_WIDTH = ATTN_HEADS * HEAD_DIM
CONV_KERNEL = 31
ROPE_THETA = 10000.0
MOBA_BLOCK = 256
MOBA_TOPK = 3
PEER_HEADS = 8
PEER_NKEYS = 128
PEER_HALF = 128
PEER_TOPK = 16
LN_EPS = 1e-5

LANES = 128
SUBLANES = 8
VMEM_LIMIT_BYTES = 56 * 1024 * 1024

NEG_BIG = -1e30
CONV_HALO = 32

F32 = jnp.float32
BF16 = jnp.bfloat16


def _params(*semantics):
    return pltpu.CompilerParams(dimension_semantics=semantics,
                                vmem_limit_bytes=VMEM_LIMIT_BYTES)


def _layer_norm_rows(y, g, b):
    mu = jnp.mean(y, axis=-1, keepdims=True)
    yc = y - mu
    var = jnp.mean(yc * yc, axis=-1, keepdims=True)
    return yc * lax.rsqrt(var + LN_EPS) * g + b


def _matmul_kernel(a_ref, b_ref, o_ref):
    a = a_ref[...].astype(BF16)
    o_ref[...] = jnp.dot(a, b_ref[...], preferred_element_type=F32).astype(o_ref.dtype)


def _matmul(a, b, *, tm, tn, out_dtype, name):
    m, k = a.shape
    _, n = b.shape
    return pl.pallas_call(
        _matmul_kernel,
        grid=(m // tm, n // tn),
        in_specs=[pl.BlockSpec((tm, k), lambda i, j: (i, 0)),
                  pl.BlockSpec((k, tn), lambda i, j: (0, j))],
        out_specs=pl.BlockSpec((tm, tn), lambda i, j: (i, j)),
        out_shape=jax.ShapeDtypeStruct((m, n), out_dtype),
        compiler_params=_params("parallel", "parallel"),
        name=name,
    )(a, b)


def _rope_kernel(q_ref, qr_ref, k_ref, kr_ref, v_ref, cos_ref, sin_ref,
                 qt_ref, ko_ref, vt_ref, km_ref):
    cos = cos_ref[...]
    sin = sin_ref[...]
    qt_ref[...] = (q_ref[...] * cos + qr_ref[...] * sin).T
    k = k_ref[...] * cos + kr_ref[...] * sin
    ko_ref[...] = k.astype(BF16)
    vt_ref[...] = v_ref[...].T.astype(BF16)
    km_ref[0] = jnp.mean(k, axis=0, keepdims=True)


def _rope(proj, cos, sin, seq):
    t = proj.shape[0]
    w = ATTN_WIDTH
    nblk = t // MOBA_BLOCK
    per_seq = seq // MOBA_BLOCK
    col = lambda c: pl.BlockSpec((MOBA_BLOCK, w), lambda i, c=c: (i, c))
    tab = pl.BlockSpec((MOBA_BLOCK, w), lambda i: (i % per_seq, 0))
    row = pl.BlockSpec((MOBA_BLOCK, w), lambda i: (i, 0))
    colmajor = pl.BlockSpec((w, MOBA_BLOCK), lambda i: (0, i))
    return pl.pallas_call(
        _rope_kernel,
        grid=(nblk,),
        in_specs=[col(0), col(1), col(2), col(3), col(4), tab, tab],
        out_specs=[colmajor, row, colmajor, pl.BlockSpec((1, 1, w), lambda i: (i, 0, 0))],
        out_shape=[jax.ShapeDtypeStruct((w, t), F32),
                   jax.ShapeDtypeStruct((t, w), BF16),
                   jax.ShapeDtypeStruct((w, t), BF16),
                   jax.ShapeDtypeStruct((nblk, 1, w), F32)],
        compiler_params=_params("parallel"),
        name="rope_kmean",
    )(proj, proj, proj, proj, proj, cos, sin)


def _tree_reduce(op, s):
    r = s.shape[0]
    while r > SUBLANES:
        r //= 2
        s = op(s[0:r, :], s[r:2 * r, :])
    if op is jnp.add:
        return jnp.sum(s, axis=0, keepdims=True)
    return jnp.max(s, axis=0, keepdims=True)


def _moba_kernel(qt_ref, k_ref, vt_ref, km_ref, e_ref, o_ref, acc0, acc1, *, scale, n_sel):
    blk = MOBA_BLOCK
    nb = km_ref.shape[1]
    i = pl.program_id(2)
    qt = qt_ref[...]
    km = km_ref[0]
    chan = lax.broadcasted_iota(jnp.int32, (LANES, blk), 0)
    km_chan = lax.broadcasted_iota(jnp.int32, (nb, LANES), 1)
    blk_ix = lax.broadcasted_iota(jnp.int32, (nb, blk), 0)
    accs = (acc0, acc1)

    qexts = []
    for h in range(2):
        kmh = jnp.where(km_chan // HEAD_DIM == h, km, 0.0)
        gate = jnp.dot(kmh, qt, precision=lax.Precision.HIGHEST,
                       preferred_element_type=F32)
        g = jnp.where(blk_ix < i, gate, -jnp.inf)
        bias = jnp.full((nb, blk), NEG_BIG, F32)
        for _ in range(n_sel):
            mx = jnp.max(g, axis=0, keepdims=True)
            first = jnp.min(jnp.where(g == mx, blk_ix, nb), axis=0, keepdims=True)
            first = jnp.where(mx > -jnp.inf, first, nb)
            pick = blk_ix == first
            bias = jnp.where(pick, 0.0, bias)
            g = jnp.where(pick, -jnp.inf, g)
        qh = (jnp.where(chan // HEAD_DIM == h, qt, 0.0) * scale).astype(BF16)
        pad = jnp.zeros((LANES - nb, blk), BF16)
        qexts.append(jnp.concatenate([qh, bias.astype(BF16), pad], axis=0))

    own = pl.multiple_of(i * blk, blk)
    kd = k_ref[pl.ds(own, blk), :]
    vd = vt_ref[:, pl.ds(own, blk)]
    key_ix = lax.broadcasted_iota(jnp.int32, (blk, blk), 0)
    qry_ix = lax.broadcasted_iota(jnp.int32, (blk, blk), 1)
    stats = []
    ss = [jnp.dot(kd, qexts[h][0:LANES, :], preferred_element_type=F32) for h in range(2)]
    for h in range(2):
        s = jnp.where(key_ix <= qry_ix, ss[h], NEG_BIG)
        m = _tree_reduce(jnp.maximum, s)
        p = jnp.exp(s - m)
        stats += [m, _tree_reduce(jnp.add, p)]
        accs[h][...] = jnp.dot(vd, p.astype(BF16), preferred_element_type=F32)

    def body(j, carry):
        off = pl.multiple_of(j * blk, blk)
        kext = jnp.concatenate([k_ref[pl.ds(off, blk), :], e_ref[pl.ds(off, blk), :]], axis=1)
        vj = vt_ref[:, pl.ds(off, blk)]
        ss = [jnp.dot(kext, qexts[h], preferred_element_type=F32) for h in range(2)]
        out = []
        for h in range(2):
            m_prev, l_prev = carry[2 * h], carry[2 * h + 1]
            s = ss[h]
            m_new = jnp.maximum(m_prev, _tree_reduce(jnp.maximum, s))
            alpha = jnp.exp(m_prev - m_new)
            p = jnp.exp(s - m_new)
            accs[h][...] = alpha * accs[h][...] + jnp.dot(vj, p.astype(BF16),
                                                          preferred_element_type=F32)
            out += [m_new, alpha * l_prev + _tree_reduce(jnp.add, p)]
        return tuple(out)

    stats = lax.fori_loop(0, i, body, tuple(stats))
    o0 = acc0[0:HEAD_DIM, :] / stats[1]
    o1 = acc1[HEAD_DIM:, :] / stats[3]
    o_ref[...] = jnp.concatenate([o0, o1], axis=0).T.astype(o_ref.dtype)


def _moba(qt, kb, vt, kmean, eye, batch, seq):
    t = kb.shape[0]
    blk = MOBA_BLOCK
    nb = seq // blk
    n_sel = max(1, min(MOBA_TOPK, nb - 1))
    pairs = ATTN_WIDTH // LANES
    kernel = functools.partial(_moba_kernel, scale=HEAD_DIM ** -0.5, n_sel=n_sel)
    return pl.pallas_call(
        kernel,
        grid=(batch, pairs, nb),
        in_specs=[pl.BlockSpec((LANES, blk), lambda b, p, i: (p, b * nb + i)),
                  pl.BlockSpec((seq, LANES), lambda b, p, i: (b, p)),
                  pl.BlockSpec((LANES, seq), lambda b, p, i: (p, b)),
                  pl.BlockSpec((1, nb, LANES), lambda b, p, i: (b, 0, p)),
                  pl.BlockSpec((seq, LANES), lambda b, p, i: (0, 0))],
        out_specs=pl.BlockSpec((blk, LANES), lambda b, p, i: (b * nb + i, p)),
        out_shape=jax.ShapeDtypeStruct((t, ATTN_WIDTH), BF16),
        scratch_shapes=[pltpu.VMEM((LANES, blk), F32), pltpu.VMEM((LANES, blk), F32)],
        compiler_params=_params("parallel", "parallel", "arbitrary"),
        name="moba_attention",
    )(qt, kb, vt, kmean, eye)


def _conv_kernel(a_ref, g_ref, ah_ref, gh_ref, w_ref, b_ref, lg_ref, lb_ref, o_ref, u_scr,
                 *, ts):
    j = pl.program_id(1)
    halo = ah_ref[...] * jax.nn.sigmoid(gh_ref[...])
    u_scr[0:CONV_HALO, :] = jnp.where(j > 0, halo, 0.0)
    u_scr[CONV_HALO:, :] = a_ref[...] * jax.nn.sigmoid(g_ref[...])
    first = CONV_HALO - (CONV_KERNEL - 1)
    acc = jnp.zeros((ts, a_ref.shape[1]), F32)
    for tap in range(CONV_KERNEL):
        acc = acc + w_ref[tap:tap + 1, :] * u_scr[first + tap:first + tap + ts, :]
    y = _layer_norm_rows(acc + b_ref[...], lg_ref[...], lb_ref[...])
    o_ref[...] = (y * jax.nn.sigmoid(y)).astype(o_ref.dtype)


def _conv(proj, a_col, g_col, w, b, ln_g, ln_b, batch, seq, ts):
    t = proj.shape[0]
    cw = w.shape[1]
    per_seq = seq // ts
    halo_per_tile = ts // CONV_HALO
    cur = lambda c: pl.BlockSpec((ts, cw), lambda bi, j, c=c: (bi * per_seq + j, c))
    halo = lambda c: pl.BlockSpec(
        (CONV_HALO, cw),
        lambda bi, j, c=c: (jnp.maximum((bi * per_seq + j) * halo_per_tile - 1, 0), c))
    vec = pl.BlockSpec((1, cw), lambda bi, j: (0, 0))
    return pl.pallas_call(
        functools.partial(_conv_kernel, ts=ts),
        grid=(batch, per_seq),
        in_specs=[cur(a_col), cur(g_col), halo(a_col), halo(g_col),
                  pl.BlockSpec((CONV_KERNEL, cw), lambda bi, j: (0, 0)), vec, vec, vec],
        out_specs=pl.BlockSpec((ts, cw), lambda bi, j: (bi * per_seq + j, 0)),
        out_shape=jax.ShapeDtypeStruct((t, cw), BF16),
        scratch_shapes=[pltpu.VMEM((CONV_HALO + ts, cw), F32)],
        compiler_params=_params("parallel", "parallel"),
        name="conformer_conv",
    )(proj, proj, proj, proj, w, b.reshape(1, cw), ln_g.reshape(1, cw), ln_b.reshape(1, cw))


def _outproj_kernel(at_ref, cv_ref, x_ref, w_ref, g_ref, b_ref, o_ref, ot_ref, *, alpha):
    aw = at_ref.shape[1]
    mixed = jnp.dot(at_ref[...], w_ref[0:aw, :], preferred_element_type=F32)
    mixed = mixed + jnp.dot(cv_ref[...], w_ref[aw:, :], preferred_element_type=F32)
    y = _layer_norm_rows(alpha * x_ref[...] + mixed, g_ref[...], b_ref[...])
    o_ref[...] = y
    ot_ref[...] = y.T.astype(BF16)


def _outproj(attn, conv, x, w_out, g, b, alpha, tm):
    t, d = x.shape
    aw = attn.shape[1]
    cw = conv.shape[1]
    vec = pl.BlockSpec((1, d), lambda i: (0, 0))
    return pl.pallas_call(
        functools.partial(_outproj_kernel, alpha=alpha),
        grid=(t // tm,),
        in_specs=[pl.BlockSpec((tm, aw), lambda i: (i, 0)),
                  pl.BlockSpec((tm, cw), lambda i: (i, 0)),
                  pl.BlockSpec((tm, d), lambda i: (i, 0)),
                  pl.BlockSpec((aw + cw, d), lambda i: (0, 0)), vec, vec],
        out_specs=[pl.BlockSpec((tm, d), lambda i: (i, 0)),
                   pl.BlockSpec((d, tm), lambda i: (0, i))],
        out_shape=[jax.ShapeDtypeStruct((t, d), F32),
                   jax.ShapeDtypeStruct((d, t), BF16)],
        compiler_params=_params("parallel"),
        name="outproj_ln",
    )(attn, conv, x, w_out, g.reshape(1, d), b.reshape(1, d))


def _fold_kernel(k_ref, w_ref, o_ref):
    o_ref[...] = lax.dot_general(k_ref[0], w_ref[...], (((1,), (1,)), ((), ())),
                                 precision=lax.Precision.HIGHEST,
                                 preferred_element_type=F32).astype(o_ref.dtype)


def _fold_keys(keys, wq):
    d = wq.shape[0]
    groups = keys.shape[0]
    return pl.pallas_call(
        _fold_kernel,
        grid=(groups,),
        in_specs=[pl.BlockSpec((1, PEER_NKEYS, PEER_HALF), lambda i: (i, 0, 0)),
                  pl.BlockSpec((d, PEER_HALF), lambda i: (0, i))],
        out_specs=pl.BlockSpec((PEER_NKEYS, d), lambda i: (i, 0)),
        out_shape=jax.ShapeDtypeStruct((groups * PEER_NKEYS, d), BF16),
        compiler_params=_params("parallel"),
        name="peer_fold_keys",
    )(keys, wq)


assert PEER_TOPK == 2 * SUBLANES


def _top_values_and_ranks(s, k, want_rank):
    rank = jnp.full(s.shape, float(k), F32) if want_rank else None
    cur = s
    tops = []
    for r in range(k):
        mx = jnp.max(cur, axis=0, keepdims=True)
        hit = cur == mx
        if want_rank:
            rank = jnp.where(hit, float(r), rank)
        cur = jnp.where(hit, -jnp.inf, cur)
        tops.append(mx)
    return jnp.concatenate(tops, axis=0), rank


def _route_kernel(s_ref, r2_ref, e2_ref, cnt_ref, e1_ref):
    nk = PEER_NKEYS
    k = PEER_TOPK
    sub = SUBLANES
    n = s_ref.shape[1]
    row = lax.broadcasted_iota(jnp.int32, (sub, n), 0)

    def head(h, carry):
        base = pl.multiple_of(h * 2 * nk, 2 * nk)
        s1 = s_ref[pl.ds(base, nk), :]
        s2 = s_ref[pl.ds(base + nk, nk), :]
        a, _ = _top_values_and_ranks(s1, k, False)
        b, rank2 = _top_values_and_ranks(s2, k, True)
        cands = [a + b[0:1, :]]
        for jj in range(1, sub):
            keep = k // (jj + 1)
            v = a[0:sub, :] + b[jj:jj + 1, :]
            cands.append(v if keep >= sub else jnp.where(row < keep, v, -jnp.inf))
        cands.append(a[0:1, :] + b[sub:k, :])
        z = jnp.concatenate(cands, axis=0)
        cur = z
        for _ in range(k):
            mx = jnp.max(cur, axis=0, keepdims=True)
            cur = jnp.where(cur == mx, -jnp.inf, cur)
        chosen = cur != z
        zmax = a[0:1, :] + b[0:1, :]
        denom = jnp.sum(jnp.where(chosen, jnp.exp(z - zmax), 0.0), axis=0, keepdims=True)
        ones = jnp.where(chosen, 1.0, 0.0)
        low = ones[0:sub, :]
        for jj in range(1, sub):
            low = low + ones[k + (jj - 1) * sub:k + jj * sub, :]
        tail = jnp.sum(ones[k + (sub - 1) * sub:, :], axis=0, keepdims=True)
        low = low + jnp.where(row == 0, tail, 0.0)
        cnt_sorted = jnp.concatenate([low, ones[sub:k, :]], axis=0)
        cnt = jnp.zeros(s1.shape, F32)
        for i in range(k):
            cnt = jnp.where(s1 == a[i:i + 1, :], cnt_sorted[i:i + 1, :], cnt)
        out = pl.multiple_of(h * nk, nk)
        r2_ref[pl.ds(out, nk), :] = rank2.astype(BF16)
        e2_ref[pl.ds(out, nk), :] = jnp.exp(s2 - b[0:1, :]).astype(BF16)
        cnt_ref[pl.ds(out, nk), :] = cnt
        e1_ref[pl.ds(out, nk), :] = jnp.exp(s1 - a[0:1, :]) * (1.0 / denom)
        return carry

    lax.fori_loop(0, PEER_HEADS, head, 0)


def _route(scores_t, tl):
    rows, t = scores_t.shape
    out_rows = rows // 2
    spec = pl.BlockSpec((out_rows, tl), lambda i: (0, i))
    shape = lambda dtype: jax.ShapeDtypeStruct((out_rows, t), dtype)
    return pl.pallas_call(
        _route_kernel,
        grid=(t // tl,),
        in_specs=[pl.BlockSpec((rows, tl), lambda i: (0, i))],
        out_specs=[spec, spec, spec, spec],
        out_shape=[shape(BF16), shape(BF16), shape(F32), shape(F32)],
        compiler_params=_params("parallel"),
        name="peer_route",
    )(scores_t)


_GATE_ROWS = 16
_GATE_LANES = 256


def _gate_tile(ht_scr, a_scr, r2_ref, e2_ref, cnt_ref, e1_ref, first_key, groups):
    nk = PEER_NKEYS
    tt = ht_scr.shape[1]
    shape = (_GATE_ROWS, _GATE_LANES)
    for c in range(groups):
        key = first_key + c
        for lh in range(tt // _GATE_LANES):
            ls = pl.ds(lh * _GATE_LANES, _GATE_LANES)
            cnt_b = [jnp.broadcast_to(cnt_ref[h, key:key + 1, ls], shape).astype(BF16)
                     for h in range(PEER_HEADS)]
            e1_b = [jnp.broadcast_to(e1_ref[h, key:key + 1, ls], shape).astype(BF16)
                    for h in range(PEER_HEADS)]
            for r in range(nk // _GATE_ROWS):
                gate = None
                for h in range(PEER_HEADS):
                    rs = pl.ds(h * nk + r * _GATE_ROWS, _GATE_ROWS)
                    e2 = e2_ref[rs, ls]
                    term = jnp.where(r2_ref[rs, ls] < cnt_b[h], e2, jnp.zeros_like(e2)) * e1_b[h]
                    gate = term if gate is None else gate + term
                hs = pl.ds(c * nk + r * _GATE_ROWS, _GATE_ROWS)
                hv = ht_scr[hs, ls]
                half = 0.5 * hv
                act = half + half * lax.erf(hv * (2.0 ** -0.5))
                a_scr[hs, ls] = gate * act.astype(BF16)


def _peer_kernel(u_ref, xt_ref, vt_ref, r2_ref, e2_ref, cnt_prev, cnt_cur, e1_prev, e1_cur,
                 x_ref, g_ref, b_ref, o_ref, acc_scr, ht_a, ht_b, a_a, a_b, *, alpha):
    tn = ht_a.shape[0]
    groups = tn // PEER_NKEYS
    p = pl.program_id(1)

    @pl.when(p == 0)
    def _():
        acc_scr[...] = jnp.zeros_like(acc_scr)
        ht_b[...] = jnp.zeros_like(ht_b)
        a_a[...] = jnp.zeros_like(a_a)

    xt = xt_ref[...]
    acc_scr[...] += jnp.dot(vt_ref[:, 0:tn], a_a[...], preferred_element_type=F32)
    _gate_tile(ht_b, a_b, r2_ref, e2_ref, cnt_prev, e1_prev, groups, groups)
    ht_a[...] = jnp.dot(u_ref[0:tn, :], xt, preferred_element_type=F32)

    acc_scr[...] += jnp.dot(vt_ref[:, tn:2 * tn], a_b[...], preferred_element_type=F32)
    _gate_tile(ht_a, a_a, r2_ref, e2_ref, cnt_cur, e1_cur, 0, groups)
    ht_b[...] = jnp.dot(u_ref[tn:2 * tn, :], xt, preferred_element_type=F32)

    @pl.when(p == pl.num_programs(1) - 1)
    def _():
        y = alpha * x_ref[...] + acc_scr[...].T
        o_ref[...] = _layer_norm_rows(y, g_ref[...], b_ref[...])


def _peer(u_b, x_t, v_t, r2, e2, cnt, e1, x, g, b, alpha, tt, tn):
    n, d = u_b.shape
    t = x.shape[0]
    n_pairs = n // (2 * tn)
    groups = tn // PEER_NKEYS
    cnt3 = cnt.reshape(PEER_HEADS, PEER_NKEYS, t)
    e13 = e1.reshape(PEER_HEADS, PEER_NKEYS, t)
    this_pair = lambda p: jnp.minimum(p, n_pairs - 1)
    last_pair = lambda p: jnp.maximum(p - 1, 0)
    tok = pl.BlockSpec((PEER_HEADS * PEER_NKEYS, tt), lambda j, p: (0, j))
    keys = lambda which: pl.BlockSpec((PEER_HEADS, 2 * groups, tt), lambda j, p: (0, which(p), j))
    vec = pl.BlockSpec((1, d), lambda j, p: (0, 0))
    return pl.pallas_call(
        functools.partial(_peer_kernel, alpha=alpha),
        grid=(t // tt, n_pairs + 1),
        in_specs=[pl.BlockSpec((2 * tn, d), lambda j, p: (this_pair(p), 0)),
                  pl.BlockSpec((d, tt), lambda j, p: (0, j)),
                  pl.BlockSpec((d, 2 * tn), lambda j, p: (0, last_pair(p))),
                  tok, tok, keys(last_pair), keys(this_pair), keys(last_pair), keys(this_pair),
                  pl.BlockSpec((tt, d), lambda j, p: (j, 0)), vec, vec],
        out_specs=pl.BlockSpec((tt, d), lambda j, p: (j, 0)),
        out_shape=jax.ShapeDtypeStruct((t, d), F32),
        scratch_shapes=[pltpu.VMEM((d, tt), F32),
                        pltpu.VMEM((tn, tt), F32), pltpu.VMEM((tn, tt), F32),
                        pltpu.VMEM((tn, tt), BF16), pltpu.VMEM((tn, tt), BF16)],
        compiler_params=_params("parallel", "arbitrary"),
        name="peer_experts",
    )(u_b, x_t, v_t, r2, e2, cnt3, cnt3, e13, e13, x, g.reshape(1, d), b.reshape(1, d))


def _rotate_half_columns(w):
    d, n = w.shape
    wh = w.reshape(d, n // HEAD_DIM, 2, HEAD_DIM // 2)
    return jnp.stack([-wh[:, :, 1], wh[:, :, 0]], axis=2).reshape(d, n)


def _rope_tables(seq):
    pos = jnp.arange(seq, dtype=F32)
    inv = jnp.power(ROPE_THETA, -jnp.arange(0, HEAD_DIM, 2, dtype=F32) / HEAD_DIM)
    ang = pos[:, None] * inv[None, :]
    cos = jnp.tile(jnp.cos(ang), (1, 2 * ATTN_HEADS))
    sin = jnp.tile(jnp.sin(ang), (1, 2 * ATTN_HEADS))
    return cos, sin


def kernel(x, w_in, conv_w, conv_b, conv_ln_g, conv_ln_b, w_out, ln1_g, ln1_b,
           peer_wq, peer_keys, peer_u, peer_v, ln2_g, ln2_b):
    batch, seq, d = x.shape
    depth = w_in.shape[0]
    t = batch * seq
    aw = ATTN_WIDTH
    alpha = (2 * depth) ** 0.25
    nb = seq // MOBA_BLOCK

    cos, sin = _rope_tables(seq)
    eye = (jnp.arange(seq)[:, None] // MOBA_BLOCK == jnp.arange(LANES)[None, :]).astype(BF16)

    xf = x.reshape(t, d)
    for l in range(depth):
        wq, wk, rest = w_in[l][:, :aw], w_in[l][:, aw:2 * aw], w_in[l][:, 2 * aw:]
        w_ext = jnp.concatenate(
            [wq, _rotate_half_columns(wq), wk, _rotate_half_columns(wk), rest], axis=1).astype(BF16)
        proj = _matmul(xf, w_ext, tm=256, tn=w_ext.shape[1], out_dtype=F32, name="in_proj")
        qt, kb, vt, kmean = _rope(proj, cos, sin, seq)
        attn = _moba(qt, kb, vt, kmean.reshape(batch, nb, aw), eye, batch, seq)
        conv = _conv(proj, 5, 6, conv_w[l], conv_b[l], conv_ln_g[l], conv_ln_b[l],
                     batch, seq, ts=512)
        x1, x1t = _outproj(attn, conv, xf, w_out[l].astype(BF16), ln1_g[l], ln1_b[l], alpha, tm=256)

        groups = PEER_HEADS * 2
        ws_t = _fold_keys(peer_keys[l].reshape(groups, PEER_NKEYS, PEER_HALF), peer_wq[l])
        scores_t = _matmul(ws_t, x1t, tm=ws_t.shape[0], tn=512, out_dtype=F32, name="peer_scores")
        r2, e2, cnt, e1 = _route(scores_t, tl=256)
        xf = _peer(peer_u[l].astype(BF16), x1t, peer_v[l].T.astype(BF16),
                   r2, e2, cnt, e1, x1, ln2_g[l], ln2_b[l], alpha, tt=512, tn=512)
    return xf.reshape(batch, seq, d)
```

```python
import functools

import jax
import jax.numpy as jnp
from jax import lax
from jax.experimental import pallas as pl
from jax.experimental.pallas import tpu as pltpu

ATTN_HEADS = 8
HEAD_DIM = 64
ATTN_WIDTH = ATTN_HEADS * HEAD_DIM
CONV_KERNEL = 31
ROPE_THETA = 10000.0
MOBA_BLOCK = 256
MOBA_TOPK = 3
PEER_HEADS = 8
PEER_NKEYS = 128
PEER_HALF = 128
PEER_TOPK = 16
LN_EPS = 1e-5

LANES = 128
SUBLANES = 8
VMEM_LIMIT_BYTES = 56 * 1024 * 1024

NEG_BIG = -1e30
CONV_HALO = 32

F32 = jnp.float32
BF16 = jnp.bfloat16


def _params(*semantics):
    return pltpu.CompilerParams(dimension_semantics=semantics,
                                vmem_limit_bytes=VMEM_LIMIT_BYTES)


def _layer_norm_rows(y, g, b):
    mu = jnp.mean(y, axis=-1, keepdims=True)
    yc = y - mu
    var = jnp.mean(yc * yc, axis=-1, keepdims=True)
    return yc * lax.rsqrt(var + LN_EPS) * g + b


def _matmul_kernel(a_ref, b_ref, o_ref):
    a = a_ref[...].astype(BF16)
    o_ref[...] = jnp.dot(a, b_ref[...], preferred_element_type=F32).astype(o_ref.dtype)


def _matmul(a, b, *, tm, tn, out_dtype, name):
    m, k = a.shape
    _, n = b.shape
    return pl.pallas_call(
        _matmul_kernel,
        grid=(m // tm, n // tn),
        in_specs=[pl.BlockSpec((tm, k), lambda i, j: (i, 0)),
                  pl.BlockSpec((k, tn), lambda i, j: (0, j))],
        out_specs=pl.BlockSpec((tm, tn), lambda i, j: (i, j)),
        out_shape=jax.ShapeDtypeStruct((m, n), out_dtype),
        compiler_params=_params("parallel", "parallel"),
        name=name,
    )(a, b)


def _rope_kernel(q_ref, qr_ref, k_ref, kr_ref, v_ref, cos_ref, sin_ref,
                 qt_ref, ko_ref, vt_ref, km_ref):
    cos = cos_ref[...]
    sin = sin_ref[...]
    qt_ref[...] = (q_ref[...] * cos + qr_ref[...] * sin).T
    k = k_ref[...] * cos + kr_ref[...] * sin
    ko_ref[...] = k.astype(BF16)
    vt_ref[...] = v_ref[...].T.astype(BF16)
    km_ref[0] = jnp.mean(k, axis=0, keepdims=True)


def _rope(proj, cos, sin, seq):
    t = proj.shape[0]
    w = ATTN_WIDTH
    nblk = t // MOBA_BLOCK
    per_seq = seq // MOBA_BLOCK
    col = lambda c: pl.BlockSpec((MOBA_BLOCK, w), lambda i, c=c: (i, c))
    tab = pl.BlockSpec((MOBA_BLOCK, w), lambda i: (i % per_seq, 0))
    row = pl.BlockSpec((MOBA_BLOCK, w), lambda i: (i, 0))
    colmajor = pl.BlockSpec((w, MOBA_BLOCK), lambda i: (0, i))
    return pl.pallas_call(
        _rope_kernel,
        grid=(nblk,),
        in_specs=[col(0), col(1), col(2), col(3), col(4), tab, tab],
        out_specs=[colmajor, row, colmajor, pl.BlockSpec((1, 1, w), lambda i: (i, 0, 0))],
        out_shape=[jax.ShapeDtypeStruct((w, t), F32),
                   jax.ShapeDtypeStruct((t, w), BF16),
                   jax.ShapeDtypeStruct((w, t), BF16),
                   jax.ShapeDtypeStruct((nblk, 1, w), F32)],
        compiler_params=_params("parallel"),
        name="rope_kmean",
    )(proj, proj, proj, proj, proj, cos, sin)


def _tree_reduce(op, s):
    r = s.shape[0]
    while r > SUBLANES:
        r //= 2
        s = op(s[0:r, :], s[r:2 * r, :])
    if op is jnp.add:
        return jnp.sum(s, axis=0, keepdims=True)
    return jnp.max(s, axis=0, keepdims=True)


def _moba_kernel(qt_ref, k_ref, vt_ref, km_ref, e_ref, o_ref, acc0, acc1, sa0, sa1, sb0, sb1,
                 *, scale, n_sel):
    blk = MOBA_BLOCK
    nb = km_ref.shape[1]
    seq = k_ref.shape[0]
    i = pl.program_id(2)
    qt = qt_ref[...]
    km = km_ref[0]
    chan = lax.broadcasted_iota(jnp.int32, (LANES, blk), 0)
    km_chan = lax.broadcasted_iota(jnp.int32, (nb, LANES), 1)
    blk_ix = lax.broadcasted_iota(jnp.int32, (nb, blk), 0)
    pad_ix = lax.broadcasted_iota(jnp.int32, (LANES - nb, blk), 0)
    accs = (acc0, acc1)
    slots = ((sa0, sa1), (sb0, sb1))

    qexts = []
    for h in range(2):
        kmh = jnp.where(km_chan // HEAD_DIM == h, km, 0.0)
        gate = jnp.dot(kmh, qt, precision=lax.Precision.HIGHEST,
                       preferred_element_type=F32)
        g = jnp.where(blk_ix < i, gate, -jnp.inf)
        bias = jnp.full((nb, blk), NEG_BIG, F32)
        for _ in range(n_sel):
            mx = jnp.max(g, axis=0, keepdims=True)
            first = jnp.min(jnp.where(g == mx, blk_ix, nb), axis=0, keepdims=True)
            first = jnp.where(mx > -jnp.inf, first, nb)
            pick = blk_ix == first
            bias = jnp.where(pick, 0.0, bias)
            g = jnp.where(pick, -jnp.inf, g)
        qh = (jnp.where(chan // HEAD_DIM == h, qt, 0.0) * scale).astype(BF16)
        pad = jnp.where(pad_ix == 0, NEG_BIG, 0.0).astype(BF16)
        qexts.append(jnp.concatenate([qh, bias.astype(BF16), pad], axis=0))

    own = pl.multiple_of(i * blk, blk)
    kd = k_ref[pl.ds(own, blk), :]
    vd = vt_ref[:, pl.ds(own, blk)]
    key_ix = lax.broadcasted_iota(jnp.int32, (blk, blk), 0)
    qry_ix = lax.broadcasted_iota(jnp.int32, (blk, blk), 1)
    stats = []
    ss = [jnp.dot(kd, qexts[h][0:LANES, :], preferred_element_type=F32) for h in range(2)]
    for h in range(2):
        s = jnp.where(key_ix <= qry_ix, ss[h], NEG_BIG)
        m = _tree_reduce(jnp.maximum, s)
        p = jnp.exp(s - m)
        stats += [m, _tree_reduce(jnp.add, p)]
        accs[h][...] = jnp.dot(vd, p.astype(BF16), preferred_element_type=F32)

    def key_offset(j):
        return pl.multiple_of(jnp.where(j < i, j, 0) * blk, blk)

    def scores(j):
        e_off = pl.multiple_of(jnp.where(j < i, j * blk, seq), blk)
        kext = jnp.concatenate([k_ref[pl.ds(key_offset(j), blk), :], e_ref[pl.ds(e_off, blk), :]],
                               axis=1)
        return [jnp.dot(kext, qexts[h], preferred_element_type=F32) for h in range(2)]

    def consume(slot, j, carry):
        vj = vt_ref[:, pl.ds(key_offset(j), blk)]
        out = []
        for h in range(2):
            m_prev, l_prev = carry[2 * h], carry[2 * h + 1]
            s = slot[h][...]
            m_new = jnp.maximum(m_prev, _tree_reduce(jnp.maximum, s))
            alpha = jnp.exp(m_prev - m_new)
            p = jnp.exp(s - m_new)
            accs[h][...] = alpha * accs[h][...] + jnp.dot(vj, p.astype(BF16),
                                                          preferred_element_type=F32)
            out += [m_new, alpha * l_prev + _tree_reduce(jnp.add, p)]
        return tuple(out)

    def fill(slot, ss):
        for h in range(2):
            slot[h][...] = ss[h]

    fill(slots[0], scores(0))

    def body(jj, carry):
        j0 = 2 * jj
        nxt = scores(j0 + 1)
        carry = consume(slots[0], j0, carry)
        fill(slots[1], nxt)
        nxt = scores(j0 + 2)
        carry = consume(slots[1], j0 + 1, carry)
        fill(slots[0], nxt)
        return carry

    stats = lax.fori_loop(0, (i + 1) // 2, body, tuple(stats))
    o0 = acc0[0:HEAD_DIM, :] / stats[1]
    o1 = acc1[HEAD_DIM:, :] / stats[3]
    o_ref[...] = jnp.concatenate([o0, o1], axis=0).T.astype(o_ref.dtype)


def _moba(qt, kb, vt, kmean, eye, batch, seq):
    t = kb.shape[0]
    blk = MOBA_BLOCK
    nb = seq // blk
    n_sel = max(1, min(MOBA_TOPK, nb - 1))
    pairs = ATTN_WIDTH // LANES
    kernel = functools.partial(_moba_kernel, scale=HEAD_DIM ** -0.5, n_sel=n_sel)
    score_buf = pltpu.VMEM((blk, blk), F32)
    return pl.pallas_call(
        kernel,
        grid=(batch, pairs, nb),
        in_specs=[pl.BlockSpec((LANES, blk), lambda b, p, i: (p, b * nb + i)),
                  pl.BlockSpec((seq, LANES), lambda b, p, i: (b, p)),
                  pl.BlockSpec((LANES, seq), lambda b, p, i: (p, b)),
                  pl.BlockSpec((1, nb, LANES), lambda b, p, i: (b, 0, p)),
                  pl.BlockSpec((seq + blk, LANES), lambda b, p, i: (0, 0))],
        out_specs=pl.BlockSpec((blk, LANES), lambda b, p, i: (b * nb + i, p)),
        out_shape=jax.ShapeDtypeStruct((t, ATTN_WIDTH), BF16),
        scratch_shapes=[pltpu.VMEM((LANES, blk), F32), pltpu.VMEM((LANES, blk), F32),
                        score_buf, score_buf, score_buf, score_buf],
        compiler_params=_params("parallel", "parallel", "arbitrary"),
        name="moba_attention",
    )(qt, kb, vt, kmean, eye)


def _conv_kernel(a_ref, g_ref, ah_ref, gh_ref, w_ref, b_ref, lg_ref, lb_ref, o_ref, u_scr,
                 *, ts):
    j = pl.program_id(1)
    halo = ah_ref[...] * jax.nn.sigmoid(gh_ref[...])
    u_scr[0, 0:CONV_HALO, :] = jnp.where(j > 0, halo, 0.0)
    u_scr[0, CONV_HALO:, :] = a_ref[...] * jax.nn.sigmoid(g_ref[...])
    rows = CONV_HALO + ts
    for r in range(1, SUBLANES):
        u_scr[r, 0:rows - SUBLANES, :] = u_scr[0, r:r + rows - SUBLANES, :]
    first = CONV_HALO - (CONV_KERNEL - 1)
    acc = jnp.zeros((ts, a_ref.shape[1]), F32)
    for tap in range(CONV_KERNEL):
        r = (first + tap) % SUBLANES
        base = first + tap - r
        acc = acc + w_ref[tap:tap + 1, :] * u_scr[r, base:base + ts, :]
    y = _layer_norm_rows(acc + b_ref[...], lg_ref[...], lb_ref[...])
    o_ref[...] = (y * jax.nn.sigmoid(y)).astype(o_ref.dtype)


def _conv(proj, a_col, g_col, w, b, ln_g, ln_b, batch, seq, ts):
    t = proj.shape[0]
    cw = w.shape[1]
    per_seq = seq // ts
    halo_per_tile = ts // CONV_HALO
    cur = lambda c: pl.BlockSpec((ts, cw), lambda bi, j, c=c: (bi * per_seq + j, c))
    halo = lambda c: pl.BlockSpec(
        (CONV_HALO, cw),
        lambda bi, j, c=c: (jnp.maximum((bi * per_seq + j) * halo_per_tile - 1, 0), c))
    vec = pl.BlockSpec((1, cw), lambda bi, j: (0, 0))
    return pl.pallas_call(
        functools.partial(_conv_kernel, ts=ts),
        grid=(batch, per_seq),
        in_specs=[cur(a_col), cur(g_col), halo(a_col), halo(g_col),
                  pl.BlockSpec((CONV_KERNEL, cw), lambda bi, j: (0, 0)), vec, vec, vec],
        out_specs=pl.BlockSpec((ts, cw), lambda bi, j: (bi * per_seq + j, 0)),
        out_shape=jax.ShapeDtypeStruct((t, cw), BF16),
        scratch_shapes=[pltpu.VMEM((SUBLANES, CONV_HALO + ts, cw), F32)],
        compiler_params=_params("parallel", "parallel"),
        name="conformer_conv",
    )(proj, proj, proj, proj, w, b.reshape(1, cw), ln_g.reshape(1, cw), ln_b.reshape(1, cw))


def _outproj_kernel(at_ref, cv_ref, x_ref, w_ref, g_ref, b_ref, o_ref, ot_ref, *, alpha):
    aw = at_ref.shape[1]
    mixed = jnp.dot(at_ref[...], w_ref[0:aw, :], preferred_element_type=F32)
    mixed = mixed + jnp.dot(cv_ref[...], w_ref[aw:, :], preferred_element_type=F32)
    y = _layer_norm_rows(alpha * x_ref[...] + mixed, g_ref[...], b_ref[...])
    o_ref[...] = y
    ot_ref[...] = y.T.astype(BF16)


def _outproj(attn, conv, x, w_out, g, b, alpha, tm):
    t, d = x.shape
    aw = attn.shape[1]
    cw = conv.shape[1]
    vec = pl.BlockSpec((1, d), lambda i: (0, 0))
    return pl.pallas_call(
        functools.partial(_outproj_kernel, alpha=alpha),
        grid=(t // tm,),
        in_specs=[pl.BlockSpec((tm, aw), lambda i: (i, 0)),
                  pl.BlockSpec((tm, cw), lambda i: (i, 0)),
                  pl.BlockSpec((tm, d), lambda i: (i, 0)),
                  pl.BlockSpec((aw + cw, d), lambda i: (0, 0)), vec, vec],
        out_specs=[pl.BlockSpec((tm, d), lambda i: (i, 0)),
                   pl.BlockSpec((d, tm), lambda i: (0, i))],
        out_shape=[jax.ShapeDtypeStruct((t, d), F32),
                   jax.ShapeDtypeStruct((d, t), BF16)],
        compiler_params=_params("parallel"),
        name="outproj_ln",
    )(attn, conv, x, w_out, g.reshape(1, d), b.reshape(1, d))


def _fold_kernel(k_ref, w_ref, o_ref):
    o_ref[...] = lax.dot_general(k_ref[0], w_ref[...], (((1,), (1,)), ((), ())),
                                 precision=lax.Precision.HIGHEST,
                                 preferred_element_type=F32).astype(o_ref.dtype)


def _fold_keys(keys, wq):
    d = wq.shape[0]
    groups = keys.shape[0]
    return pl.pallas_call(
        _fold_kernel,
        grid=(groups,),
        in_specs=[pl.BlockSpec((1, PEER_NKEYS, PEER_HALF), lambda i: (i, 0, 0)),
                  pl.BlockSpec((d, PEER_HALF), lambda i: (0, i))],
        out_specs=pl.BlockSpec((PEER_NKEYS, d), lambda i: (i, 0)),
        out_shape=jax.ShapeDtypeStruct((groups * PEER_NKEYS, d), BF16),
        compiler_params=_params("parallel"),
        name="peer_fold_keys",
    )(keys, wq)


assert PEER_TOPK == 2 * SUBLANES


def _top_values_and_ranks(s, k, want_rank):
    rank = jnp.full(s.shape, float(k), F32) if want_rank else None
    cur = s
    tops = []
    for r in range(k):
        mx = jnp.max(cur, axis=0, keepdims=True)
        hit = cur == mx
        if want_rank:
            rank = jnp.where(hit, float(r), rank)
        cur = jnp.where(hit, -jnp.inf, cur)
        tops.append(mx)
    return jnp.concatenate(tops, axis=0), rank


def _route_kernel(s_ref, r2_ref, e2_ref, cnt_ref, e1_ref):
    nk = PEER_NKEYS
    k = PEER_TOPK
    sub = SUBLANES
    n = s_ref.shape[1]
    row = lax.broadcasted_iota(jnp.int32, (sub, n), 0)

    def head(h, carry):
        base = pl.multiple_of(h * 2 * nk, 2 * nk)
        s1 = s_ref[pl.ds(base, nk), :]
        s2 = s_ref[pl.ds(base + nk, nk), :]
        a, _ = _top_values_and_ranks(s1, k, False)
        b, rank2 = _top_values_and_ranks(s2, k, True)
        cands = [a + b[0:1, :]]
        for jj in range(1, sub):
            keep = k // (jj + 1)
            v = a[0:sub, :] + b[jj:jj + 1, :]
            cands.append(v if keep >= sub else jnp.where(row < keep, v, -jnp.inf))
        cands.append(a[0:1, :] + b[sub:k, :])
        z = jnp.concatenate(cands, axis=0)
        cur = z
        for _ in range(k):
            mx = jnp.max(cur, axis=0, keepdims=True)
            cur = jnp.where(cur == mx, -jnp.inf, cur)
        chosen = cur != z
        zmax = a[0:1, :] + b[0:1, :]
        denom = jnp.sum(jnp.where(chosen, jnp.exp(z - zmax), 0.0), axis=0, keepdims=True)
        ones = jnp.where(chosen, 1.0, 0.0)
        low = ones[0:sub, :]
        for jj in range(1, sub):
            low = low + ones[k + (jj - 1) * sub:k + jj * sub, :]
        tail = jnp.sum(ones[k + (sub - 1) * sub:, :], axis=0, keepdims=True)
        low = low + jnp.where(row == 0, tail, 0.0)
        cnt_sorted = jnp.concatenate([low, ones[sub:k, :]], axis=0)
        cnt = jnp.zeros(s1.shape, F32)
        for i in range(k):
            cnt = jnp.where(s1 == a[i:i + 1, :], cnt_sorted[i:i + 1, :], cnt)
        out = pl.multiple_of(h * nk, nk)
        r2_ref[pl.ds(out, nk), :] = rank2.astype(BF16)
        e2_ref[pl.ds(out, nk), :] = jnp.exp(s2 - b[0:1, :]).astype(BF16)
        cnt_ref[pl.ds(out, nk), :] = cnt
        e1_ref[pl.ds(out, nk), :] = jnp.exp(s1 - a[0:1, :]) * (1.0 / denom)
        return carry

    lax.fori_loop(0, PEER_HEADS, head, 0)


def _route(scores_t, tl):
    rows, t = scores_t.shape
    out_rows = rows // 2
    spec = pl.BlockSpec((out_rows, tl), lambda i: (0, i))
    shape = lambda dtype: jax.ShapeDtypeStruct((out_rows, t), dtype)
    return pl.pallas_call(
        _route_kernel,
        grid=(t // tl,),
        in_specs=[pl.BlockSpec((rows, tl), lambda i: (0, i))],
        out_specs=[spec, spec, spec, spec],
        out_shape=[shape(BF16), shape(BF16), shape(F32), shape(F32)],
        compiler_params=_params("parallel"),
        name="peer_route",
    )(scores_t)


_GATE_ROWS = 16
_GATE_LANES = 256


def _gate_tile(ht_scr, a_scr, r2_ref, e2_ref, cnt_ref, e1_ref, first_key, groups):
    nk = PEER_NKEYS
    tt = ht_scr.shape[1]
    shape = (_GATE_ROWS, _GATE_LANES)
    for c in range(groups):
        key = first_key + c
        for lh in range(tt // _GATE_LANES):
            ls = pl.ds(lh * _GATE_LANES, _GATE_LANES)
            cnt_b = [jnp.broadcast_to(cnt_ref[h, key:key + 1, ls], shape).astype(BF16)
                     for h in range(PEER_HEADS)]
            e1_b = [jnp.broadcast_to(e1_ref[h, key:key + 1, ls], shape).astype(BF16)
                    for h in range(PEER_HEADS)]
            for r in range(nk // _GATE_ROWS):
                gate = None
                for h in range(PEER_HEADS):
                    rs = pl.ds(h * nk + r * _GATE_ROWS, _GATE_ROWS)
                    e2 = e2_ref[rs, ls]
                    term = jnp.where(r2_ref[rs, ls] < cnt_b[h], e2, jnp.zeros_like(e2)) * e1_b[h]
                    gate = term if gate is None else gate + term
                hs = pl.ds(c * nk + r * _GATE_ROWS, _GATE_ROWS)
                hv = ht_scr[hs, ls]
                half = 0.5 * hv
                act = half + half * lax.erf(hv * (2.0 ** -0.5))
                a_scr[hs, ls] = gate * act.astype(BF16)


def _peer_kernel(u_ref, xt_ref, vt_ref, r2_ref, e2_ref, cnt_prev, cnt_cur, e1_prev, e1_cur,
                 x_ref, g_ref, b_ref, o_ref, acc_scr, ht_a, ht_b, a_a, a_b, *, alpha):
    tn = ht_a.shape[0]
    groups = tn // PEER_NKEYS
    p = pl.program_id(1)

    @pl.when(p == 0)
    def _():
        acc_scr[...] = jnp.zeros_like(acc_scr)
        ht_b[...] = jnp.zeros_like(ht_b)
        a_a[...] = jnp.zeros_like(a_a)

    xt = xt_ref[...]
    acc_scr[...] += jnp.dot(vt_ref[:, 0:tn], a_a[...], preferred_element_type=F32)
    _gate_tile(ht_b, a_b, r2_ref, e2_ref, cnt_prev, e1_prev, groups, groups)
    ht_a[...] = jnp.dot(u_ref[0:tn, :], xt, preferred_element_type=F32)

    acc_scr[...] += jnp.dot(vt_ref[:, tn:2 * tn], a_b[...], preferred_element_type=F32)
    _gate_tile(ht_a, a_a, r2_ref, e2_ref, cnt_cur, e1_cur, 0, groups)
    ht_b[...] = jnp.dot(u_ref[tn:2 * tn, :], xt, preferred_element_type=F32)

    @pl.when(p == pl.num_programs(1) - 1)
    def _():
        y = alpha * x_ref[...] + acc_scr[...].T
        o_ref[...] = _layer_norm_rows(y, g_ref[...], b_ref[...])


def _peer(u_b, x_t, v_t, r2, e2, cnt, e1, x, g, b, alpha, tt, tn):
    n, d = u_b.shape
    t = x.shape[0]
    n_pairs = n // (2 * tn)
    groups = tn // PEER_NKEYS
    cnt3 = cnt.reshape(PEER_HEADS, PEER_NKEYS, t)
    e13 = e1.reshape(PEER_HEADS, PEER_NKEYS, t)
    this_pair = lambda p: jnp.minimum(p, n_pairs - 1)
    last_pair = lambda p: jnp.maximum(p - 1, 0)
    tok = pl.BlockSpec((PEER_HEADS * PEER_NKEYS, tt), lambda j, p: (0, j))
    keys = lambda which: pl.BlockSpec((PEER_HEADS, 2 * groups, tt), lambda j, p: (0, which(p), j))
    vec = pl.BlockSpec((1, d), lambda j, p: (0, 0))
    return pl.pallas_call(
        functools.partial(_peer_kernel, alpha=alpha),
        grid=(t // tt, n_pairs + 1),
        in_specs=[pl.BlockSpec((2 * tn, d), lambda j, p: (this_pair(p), 0)),
                  pl.BlockSpec((d, tt), lambda j, p: (0, j)),
                  pl.BlockSpec((d, 2 * tn), lambda j, p: (0, last_pair(p))),
                  tok, tok, keys(last_pair), keys(this_pair), keys(last_pair), keys(this_pair),
                  pl.BlockSpec((tt, d), lambda j, p: (j, 0)), vec, vec],
        out_specs=pl.BlockSpec((tt, d), lambda j, p: (j, 0)),
        out_shape=jax.ShapeDtypeStruct((t, d), F32),
        scratch_shapes=[pltpu.VMEM((d, tt), F32),
                        pltpu.VMEM((tn, tt), F32), pltpu.VMEM((tn, tt), F32),
                        pltpu.VMEM((tn, tt), BF16), pltpu.VMEM((tn, tt), BF16)],
        compiler_params=_params("parallel", "arbitrary"),
        name="peer_experts",
    )(u_b, x_t, v_t, r2, e2, cnt3, cnt3, e13, e13, x, g.reshape(1, d), b.reshape(1, d))


def _rotate_half_columns(w):
    d, n = w.shape
    wh = w.reshape(d, n // HEAD_DIM, 2, HEAD_DIM // 2)
    return jnp.stack([-wh[:, :, 1], wh[:, :, 0]], axis=2).reshape(d, n)


def _rope_tables(seq):
    pos = jnp.arange(seq, dtype=F32)
    inv = jnp.power(ROPE_THETA, -jnp.arange(0, HEAD_DIM, 2, dtype=F32) / HEAD_DIM)
    ang = pos[:, None] * inv[None, :]
    cos = jnp.tile(jnp.cos(ang), (1, 2 * ATTN_HEADS))
    sin = jnp.tile(jnp.sin(ang), (1, 2 * ATTN_HEADS))
    return cos, sin


def kernel(x, w_in, conv_w, conv_b, conv_ln_g, conv_ln_b, w_out, ln1_g, ln1_b,
           peer_wq, peer_keys, peer_u, peer_v, ln2_g, ln2_b):
    batch, seq, d = x.shape
    depth = w_in.shape[0]
    t = batch * seq
    aw = ATTN_WIDTH
    alpha = (2 * depth) ** 0.25
    nb = seq // MOBA_BLOCK

    cos, sin = _rope_tables(seq)
    eye = (jnp.arange(seq + MOBA_BLOCK)[:, None] // MOBA_BLOCK
           == jnp.arange(LANES)[None, :]).astype(BF16)

    xf = x.reshape(t, d)
    for l in range(depth):
        wq, wk, rest = w_in[l][:, :aw], w_in[l][:, aw:2 * aw], w_in[l][:, 2 * aw:]
        w_ext = jnp.concatenate(
            [wq, _rotate_half_columns(wq), wk, _rotate_half_columns(wk), rest], axis=1).astype(BF16)
        proj = _matmul(xf, w_ext, tm=256, tn=w_ext.shape[1], out_dtype=F32, name="in_proj")
        qt, kb, vt, kmean = _rope(proj, cos, sin, seq)
        attn = _moba(qt, kb, vt, kmean.reshape(batch, nb, aw), eye, batch, seq)
        conv = _conv(proj, 5, 6, conv_w[l], conv_b[l], conv_ln_g[l], conv_ln_b[l],
                     batch, seq, ts=512)
        x1, x1t = _outproj(attn, conv, xf, w_out[l].astype(BF16), ln1_g[l], ln1_b[l], alpha, tm=256)

        groups = PEER_HEADS * 2
        ws_t = _fold_keys(peer_keys[l].reshape(groups, PEER_NKEYS, PEER_HALF), peer_wq[l])
        scores_t = _matmul(ws_t, x1t, tm=ws_t.shape[0], tn=512, out_dtype=F32, name="peer_scores")
        r2, e2, cnt, e1 = _route(scores_t, tl=256)
        xf = _peer(peer_u[l].astype(BF16), x1t, peer_v[l].astype(BF16).T,
                   r2, e2, cnt, e1, x1, ln2_g[l], ln2_b[l], alpha, tt=512, tn=512)
    return xf.reshape(batch, seq, d)
```

```python
import functools

import jax
import jax.numpy as jnp
from jax import lax
from jax.experimental import pallas as pl
from jax.experimental.pallas import tpu as pltpu

ATTN_HEADS = 8
HEAD_DIM = 64
ATTN_WIDTH = ATTN_HEADS * HEAD_DIM
CONV_KERNEL = 31
ROPE_THETA = 10000.0
MOBA_BLOCK = 256
MOBA_TOPK = 3
PEER_HEADS = 8
PEER_NKEYS = 128
PEER_HALF = 128
PEER_TOPK = 16
LN_EPS = 1e-5

LANES = 128
SUBLANES = 8
VMEM_LIMIT_BYTES = 56 * 1024 * 1024

NEG_BIG = -1e30
CONV_HALO = 32

F32 = jnp.float32
BF16 = jnp.bfloat16


def _params(*semantics):
    return pltpu.CompilerParams(dimension_semantics=semantics,
                                vmem_limit_bytes=VMEM_LIMIT_BYTES)


def _layer_norm_rows(y, g, b):
    mu = jnp.mean(y, axis=-1, keepdims=True)
    yc = y - mu
    var = jnp.mean(yc * yc, axis=-1, keepdims=True)
    return yc * lax.rsqrt(var + LN_EPS) * g + b


def _matmul_kernel(a_ref, b_ref, o_ref):
    a = a_ref[...].astype(BF16)
    o_ref[...] = jnp.dot(a, b_ref[...], preferred_element_type=F32).astype(o_ref.dtype)


def _matmul(a, b, *, tm, tn, out_dtype, name):
    m, k = a.shape
    _, n = b.shape
    return pl.pallas_call(
        _matmul_kernel,
        grid=(m // tm, n // tn),
        in_specs=[pl.BlockSpec((tm, k), lambda i, j: (i, 0)),
                  pl.BlockSpec((k, tn), lambda i, j: (0, j))],
        out_specs=pl.BlockSpec((tm, tn), lambda i, j: (i, j)),
        out_shape=jax.ShapeDtypeStruct((m, n), out_dtype),
        compiler_params=_params("parallel", "parallel"),
        name=name,
    )(a, b)


def _rope_kernel(q_ref, qr_ref, k_ref, kr_ref, v_ref, cos_ref, sin_ref,
                 qt_ref, ko_ref, vt_ref, km_ref):
    cos = cos_ref[...]
    sin = sin_ref[...]
    qt_ref[...] = (q_ref[...] * cos + qr_ref[...] * sin).T
    k = k_ref[...] * cos + kr_ref[...] * sin
    ko_ref[...] = k.astype(BF16)
    vt_ref[...] = v_ref[...].T.astype(BF16)
    km_ref[0] = jnp.mean(k, axis=0, keepdims=True)


def _rope(proj, cos, sin, seq):
    t = proj.shape[0]
    w = ATTN_WIDTH
    nblk = t // MOBA_BLOCK
    per_seq = seq // MOBA_BLOCK
    col = lambda c: pl.BlockSpec((MOBA_BLOCK, w), lambda i, c=c: (i, c))
    tab = pl.BlockSpec((MOBA_BLOCK, w), lambda i: (i % per_seq, 0))
    row = pl.BlockSpec((MOBA_BLOCK, w), lambda i: (i, 0))
    colmajor = pl.BlockSpec((w, MOBA_BLOCK), lambda i: (0, i))
    return pl.pallas_call(
        _rope_kernel,
        grid=(nblk,),
        in_specs=[col(0), col(1), col(2), col(3), col(4), tab, tab],
        out_specs=[colmajor, row, colmajor, pl.BlockSpec((1, 1, w), lambda i: (i, 0, 0))],
        out_shape=[jax.ShapeDtypeStruct((w, t), F32),
                   jax.ShapeDtypeStruct((t, w), BF16),
                   jax.ShapeDtypeStruct((w, t), BF16),
                   jax.ShapeDtypeStruct((nblk, 1, w), F32)],
        compiler_params=_params("parallel"),
        name="rope_kmean",
    )(proj, proj, proj, proj, proj, cos, sin)


def _tree_reduce(op, s):
    r = s.shape[0]
    while r > SUBLANES:
        r //= 2
        s = op(s[0:r, :], s[r:2 * r, :])
    if op is jnp.add:
        return jnp.sum(s, axis=0, keepdims=True)
    return jnp.max(s, axis=0, keepdims=True)


def _moba_kernel(qt_ref, k_ref, vt_ref, km_ref, e_ref, o_ref, acc0, acc1, sa0, sa1, sb0, sb1,
                 *, scale, n_sel):
    blk = MOBA_BLOCK
    nb = km_ref.shape[1]
    seq = k_ref.shape[0]
    i = pl.program_id(2)
    qt = qt_ref[...]
    km = km_ref[0]
    chan = lax.broadcasted_iota(jnp.int32, (LANES, blk), 0)
    km_chan = lax.broadcasted_iota(jnp.int32, (nb, LANES), 1)
    blk_ix = lax.broadcasted_iota(jnp.int32, (nb, blk), 0)
    pad_ix = lax.broadcasted_iota(jnp.int32, (LANES - nb, blk), 0)
    accs = (acc0, acc1)
    slots = ((sa0, sa1), (sb0, sb1))

    qexts = []
    for h in range(2):
        kmh = jnp.where(km_chan // HEAD_DIM == h, km, 0.0)
        gate = jnp.dot(kmh, qt, precision=lax.Precision.HIGHEST,
                       preferred_element_type=F32)
        g = jnp.where(blk_ix < i, gate, -jnp.inf)
        bias = jnp.full((nb, blk), NEG_BIG, F32)
        for _ in range(n_sel):
            mx = jnp.max(g, axis=0, keepdims=True)
            first = jnp.min(jnp.where(g == mx, blk_ix, nb), axis=0, keepdims=True)
            first = jnp.where(mx > -jnp.inf, first, nb)
            pick = blk_ix == first
            bias = jnp.where(pick, 0.0, bias)
            g = jnp.where(pick, -jnp.inf, g)
        qh = (jnp.where(chan // HEAD_DIM == h, qt, 0.0) * scale).astype(BF16)
        pad = jnp.where(pad_ix == 0, NEG_BIG, 0.0).astype(BF16)
        qexts.append(jnp.concatenate([qh, bias.astype(BF16), pad], axis=0))

    own = pl.multiple_of(i * blk, blk)
    kd = k_ref[pl.ds(own, blk), :]
    vd = vt_ref[:, pl.ds(own, blk)]
    key_ix = lax.broadcasted_iota(jnp.int32, (blk, blk), 0)
    qry_ix = lax.broadcasted_iota(jnp.int32, (blk, blk), 1)
    stats = []
    ss = [jnp.dot(kd, qexts[h][0:LANES, :], preferred_element_type=F32) for h in range(2)]
    for h in range(2):
        s = jnp.where(key_ix <= qry_ix, ss[h], NEG_BIG)
        m = _tree_reduce(jnp.maximum, s)
        p = jnp.exp(s - m)
        stats += [m, _tree_reduce(jnp.add, p)]
        accs[h][...] = jnp.dot(vd, p.astype(BF16), preferred_element_type=F32)

    def key_offset(j):
        return pl.multiple_of(jnp.where(j < i, j, 0) * blk, blk)

    def scores(j):
        e_off = pl.multiple_of(jnp.where(j < i, j * blk, seq), blk)
        kext = jnp.concatenate([k_ref[pl.ds(key_offset(j), blk), :], e_ref[pl.ds(e_off, blk), :]],
                               axis=1)
        return [jnp.dot(kext, qexts[h], preferred_element_type=F32) for h in range(2)]

    def consume(slot, j, carry):
        vj = vt_ref[:, pl.ds(key_offset(j), blk)]
        out = []
        for h in range(2):
            m_prev, l_prev = carry[2 * h], carry[2 * h + 1]
            s = slot[h][...]
            m_new = jnp.maximum(m_prev, _tree_reduce(jnp.maximum, s))
            alpha = jnp.exp(m_prev - m_new)
            p = jnp.exp(s - m_new)
            accs[h][...] = alpha * accs[h][...] + jnp.dot(vj, p.astype(BF16),
                                                          preferred_element_type=F32)
            out += [m_new, alpha * l_prev + _tree_reduce(jnp.add, p)]
        return tuple(out)

    def fill(slot, ss):
        for h in range(2):
            slot[h][...] = ss[h]

    fill(slots[0], scores(0))

    def body(jj, carry):
        j0 = 2 * jj
        nxt = scores(j0 + 1)
        carry = consume(slots[0], j0, carry)
        fill(slots[1], nxt)
        nxt = scores(j0 + 2)
        carry = consume(slots[1], j0 + 1, carry)
        fill(slots[0], nxt)
        return carry

    stats = lax.fori_loop(0, (i + 1) // 2, body, tuple(stats))
    o0 = acc0[0:HEAD_DIM, :] / stats[1]
    o1 = acc1[HEAD_DIM:, :] / stats[3]
    o_ref[...] = jnp.concatenate([o0, o1], axis=0).T.astype(o_ref.dtype)


def _moba(qt, kb, vt, kmean, eye, batch, seq):
    t = kb.shape[0]
    blk = MOBA_BLOCK
    nb = seq // blk
    n_sel = max(1, min(MOBA_TOPK, nb - 1))
    pairs = ATTN_WIDTH // LANES
    kernel = functools.partial(_moba_kernel, scale=HEAD_DIM ** -0.5, n_sel=n_sel)
    score_buf = pltpu.VMEM((blk, blk), F32)
    return pl.pallas_call(
        kernel,
        grid=(batch, pairs, nb),
        in_specs=[pl.BlockSpec((LANES, blk), lambda b, p, i: (p, b * nb + i)),
                  pl.BlockSpec((seq, LANES), lambda b, p, i: (b, p)),
                  pl.BlockSpec((LANES, seq), lambda b, p, i: (p, b)),
                  pl.BlockSpec((1, nb, LANES), lambda b, p, i: (b, 0, p)),
                  pl.BlockSpec((seq + blk, LANES), lambda b, p, i: (0, 0))],
        out_specs=pl.BlockSpec((blk, LANES), lambda b, p, i: (b * nb + i, p)),
        out_shape=jax.ShapeDtypeStruct((t, ATTN_WIDTH), BF16),
        scratch_shapes=[pltpu.VMEM((LANES, blk), F32), pltpu.VMEM((LANES, blk), F32),
                        score_buf, score_buf, score_buf, score_buf],
        compiler_params=_params("parallel", "parallel", "arbitrary"),
        name="moba_attention",
    )(qt, kb, vt, kmean, eye)


def _conv_kernel(a_ref, g_ref, ah_ref, gh_ref, w_ref, b_ref, lg_ref, lb_ref, o_ref, u_scr,
                 *, ts):
    j = pl.program_id(1)
    halo = ah_ref[...] * jax.nn.sigmoid(gh_ref[...])
    u_scr[0, 0:CONV_HALO, :] = jnp.where(j > 0, halo, 0.0)
    u_scr[0, CONV_HALO:, :] = a_ref[...] * jax.nn.sigmoid(g_ref[...])
    rows = CONV_HALO + ts
    for r in range(1, SUBLANES):
        u_scr[r, 0:rows - SUBLANES, :] = u_scr[0, r:r + rows - SUBLANES, :]
    first = CONV_HALO - (CONV_KERNEL - 1)
    acc = jnp.zeros((ts, a_ref.shape[1]), F32)
    for tap in range(CONV_KERNEL):
        r = (first + tap) % SUBLANES
        base = first + tap - r
        acc = acc + w_ref[tap:tap + 1, :] * u_scr[r, base:base + ts, :]
    y = _layer_norm_rows(acc + b_ref[...], lg_ref[...], lb_ref[...])
    o_ref[...] = (y * jax.nn.sigmoid(y)).astype(o_ref.dtype)


def _conv(proj, a_col, g_col, w, b, ln_g, ln_b, batch, seq, ts):
    t = proj.shape[0]
    cw = w.shape[1]
    per_seq = seq // ts
    halo_per_tile = ts // CONV_HALO
    cur = lambda c: pl.BlockSpec((ts, cw), lambda bi, j, c=c: (bi * per_seq + j, c))
    halo = lambda c: pl.BlockSpec(
        (CONV_HALO, cw),
        lambda bi, j, c=c: (jnp.maximum((bi * per_seq + j) * halo_per_tile - 1, 0), c))
    vec = pl.BlockSpec((1, cw), lambda bi, j: (0, 0))
    return pl.pallas_call(
        functools.partial(_conv_kernel, ts=ts),
        grid=(batch, per_seq),
        in_specs=[cur(a_col), cur(g_col), halo(a_col), halo(g_col),
                  pl.BlockSpec((CONV_KERNEL, cw), lambda bi, j: (0, 0)), vec, vec, vec],
        out_specs=pl.BlockSpec((ts, cw), lambda bi, j: (bi * per_seq + j, 0)),
        out_shape=jax.ShapeDtypeStruct((t, cw), BF16),
        scratch_shapes=[pltpu.VMEM((SUBLANES, CONV_HALO + ts, cw), F32)],
        compiler_params=_params("parallel", "parallel"),
        name="conformer_conv",
    )(proj, proj, proj, proj, w, b.reshape(1, cw), ln_g.reshape(1, cw), ln_b.reshape(1, cw))


def _outproj_kernel(at_ref, cv_ref, x_ref, w_ref, g_ref, b_ref, o_ref, ot_ref, *, alpha):
    aw = at_ref.shape[1]
    mixed = jnp.dot(at_ref[...], w_ref[0:aw, :], preferred_element_type=F32)
    mixed = mixed + jnp.dot(cv_ref[...], w_ref[aw:, :], preferred_element_type=F32)
    y = _layer_norm_rows(alpha * x_ref[...] + mixed, g_ref[...], b_ref[...])
    o_ref[...] = y
    ot_ref[...] = y.T.astype(BF16)


def _outproj(attn, conv, x, w_out, g, b, alpha, tm):
    t, d = x.shape
    aw = attn.shape[1]
    cw = conv.shape[1]
    vec = pl.BlockSpec((1, d), lambda i: (0, 0))
    return pl.pallas_call(
        functools.partial(_outproj_kernel, alpha=alpha),
        grid=(t // tm,),
        in_specs=[pl.BlockSpec((tm, aw), lambda i: (i, 0)),
                  pl.BlockSpec((tm, cw), lambda i: (i, 0)),
                  pl.BlockSpec((tm, d), lambda i: (i, 0)),
                  pl.BlockSpec((aw + cw, d), lambda i: (0, 0)), vec, vec],
        out_specs=[pl.BlockSpec((tm, d), lambda i: (i, 0)),
                   pl.BlockSpec((d, tm), lambda i: (0, i))],
        out_shape=[jax.ShapeDtypeStruct((t, d), F32),
                   jax.ShapeDtypeStruct((d, t), BF16)],
        compiler_params=_params("parallel"),
        name="outproj_ln",
    )(attn, conv, x, w_out, g.reshape(1, d), b.reshape(1, d))


def _fold_kernel(k_ref, w_ref, o_ref):
    o_ref[...] = lax.dot_general(k_ref[0], w_ref[...], (((1,), (1,)), ((), ())),
                                 precision=lax.Precision.HIGHEST,
                                 preferred_element_type=F32).astype(o_ref.dtype)


def _fold_keys(keys, wq):
    d = wq.shape[0]
    groups = keys.shape[0]
    return pl.pallas_call(
        _fold_kernel,
        grid=(groups,),
        in_specs=[pl.BlockSpec((1, PEER_NKEYS, PEER_HALF), lambda i: (i, 0, 0)),
                  pl.BlockSpec((d, PEER_HALF), lambda i: (0, i))],
        out_specs=pl.BlockSpec((PEER_NKEYS, d), lambda i: (i, 0)),
        out_shape=jax.ShapeDtypeStruct((groups * PEER_NKEYS, d), BF16),
        compiler_params=_params("parallel"),
        name="peer_fold_keys",
    )(keys, wq)


assert PEER_TOPK == 2 * SUBLANES


def _top_values_and_ranks(s, k, want_rank):
    rank = jnp.full(s.shape, float(k), F32) if want_rank else None
    cur = s
    tops = []
    for r in range(k):
        mx = jnp.max(cur, axis=0, keepdims=True)
        hit = cur == mx
        if want_rank:
            rank = jnp.where(hit, float(r), rank)
        cur = jnp.where(hit, -jnp.inf, cur)
        tops.append(mx)
    return jnp.concatenate(tops, axis=0), rank


def _route_kernel(s_ref, r2_ref, e2_ref, cnt_ref, e1_ref):
    nk = PEER_NKEYS
    k = PEER_TOPK
    sub = SUBLANES
    n = s_ref.shape[1]
    row = lax.broadcasted_iota(jnp.int32, (sub, n), 0)

    def head(h, carry):
        base = pl.multiple_of(h * 2 * nk, 2 * nk)
        s1 = s_ref[pl.ds(base, nk), :]
        s2 = s_ref[pl.ds(base + nk, nk), :]
        a, _ = _top_values_and_ranks(s1, k, False)
        b, rank2 = _top_values_and_ranks(s2, k, True)
        cands = [a + b[0:1, :]]
        for jj in range(1, sub):
            keep = k // (jj + 1)
            v = a[0:sub, :] + b[jj:jj + 1, :]
            cands.append(v if keep >= sub else jnp.where(row < keep, v, -jnp.inf))
        cands.append(a[0:1, :] + b[sub:k, :])
        z = jnp.concatenate(cands, axis=0)
        cur = z
        for _ in range(k):
            mx = jnp.max(cur, axis=0, keepdims=True)
            cur = jnp.where(cur == mx, -jnp.inf, cur)
        chosen = cur != z
        zmax = a[0:1, :] + b[0:1, :]
        denom = jnp.sum(jnp.where(chosen, jnp.exp(z - zmax), 0.0), axis=0, keepdims=True)
        ones = jnp.where(chosen, 1.0, 0.0)
        low = ones[0:sub, :]
        for jj in range(1, sub):
            low = low + ones[k + (jj - 1) * sub:k + jj * sub, :]
        tail = jnp.sum(ones[k + (sub - 1) * sub:, :], axis=0, keepdims=True)
        low = low + jnp.where(row == 0, tail, 0.0)
        cnt_sorted = jnp.concatenate([low, ones[sub:k, :]], axis=0)
        cnt = jnp.zeros(s1.shape, F32)
        for i in range(k):
            cnt = jnp.where(s1 == a[i:i + 1, :], cnt_sorted[i:i + 1, :], cnt)
        out = pl.multiple_of(h * nk, nk)
        r2_ref[pl.ds(out, nk), :] = rank2.astype(BF16)
        e2_ref[pl.ds(out, nk), :] = jnp.exp(s2 - b[0:1, :]).astype(BF16)
        cnt_ref[pl.ds(out, nk), :] = cnt
        e1_ref[pl.ds(out, nk), :] = jnp.exp(s1 - a[0:1, :]) * (1.0 / denom)
        return carry

    lax.fori_loop(0, PEER_HEADS, head, 0)


def _route(scores_t, tl):
    rows, t = scores_t.shape
    out_rows = rows // 2
    spec = pl.BlockSpec((out_rows, tl), lambda i: (0, i))
    shape = lambda dtype: jax.ShapeDtypeStruct((out_rows, t), dtype)
    return pl.pallas_call(
        _route_kernel,
        grid=(t // tl,),
        in_specs=[pl.BlockSpec((rows, tl), lambda i: (0, i))],
        out_specs=[spec, spec, spec, spec],
        out_shape=[shape(BF16), shape(BF16), shape(F32), shape(F32)],
        compiler_params=_params("parallel"),
        name="peer_route",
    )(scores_t)


_GATE_ROWS = 16
_GATE_LANES = 256


def _gate_block(ht_scr, a_scr, r2_ref, e2_ref, cnt_ref, e1_ref, key, c, lh):
    nk = PEER_NKEYS
    shape = (_GATE_ROWS, _GATE_LANES)
    ls = pl.ds(lh * _GATE_LANES, _GATE_LANES)
    cnt_b = [jnp.broadcast_to(cnt_ref[h, key:key + 1, ls], shape).astype(BF16)
             for h in range(PEER_HEADS)]
    e1_b = [jnp.broadcast_to(e1_ref[h, key:key + 1, ls], shape).astype(BF16)
            for h in range(PEER_HEADS)]
    for r in range(nk // _GATE_ROWS):
        gate = None
        for h in range(PEER_HEADS):
            rs = pl.ds(h * nk + r * _GATE_ROWS, _GATE_ROWS)
            e2 = e2_ref[rs, ls]
            term = jnp.where(r2_ref[rs, ls] < cnt_b[h], e2, jnp.zeros_like(e2)) * e1_b[h]
            gate = term if gate is None else gate + term
        hs = pl.ds(c * nk + r * _GATE_ROWS, _GATE_ROWS)
        hv = ht_scr[hs, ls]
        half = 0.5 * hv
        act = half + half * lax.erf(hv * (2.0 ** -0.5))
        a_scr[hs, ls] = gate * act.astype(BF16)


def _peer_kernel(u_ref, xt_ref, vt_ref, r2_ref, e2_ref, cnt_prev, cnt_cur, e1_prev, e1_cur,
                 x_ref, g_ref, b_ref, o_ref, acc_scr, ht_a, ht_b, a_a, a_b, *, alpha):
    tn, tt = ht_a.shape
    groups = tn // PEER_NKEYS
    p = pl.program_id(1)

    @pl.when(p == 0)
    def _():
        acc_scr[...] = jnp.zeros_like(acc_scr)
        ht_b[...] = jnp.zeros_like(ht_b)
        a_a[...] = jnp.zeros_like(a_a)

    def half_step(u_lo, a_in, ht_in, a_out, ht_out, cnt_ref, e1_ref, first_key):
        for lh in range(tt // _GATE_LANES):
            ls = pl.ds(lh * _GATE_LANES, _GATE_LANES)
            acc_scr[:, ls] += jnp.dot(vt_ref[:, u_lo:u_lo + tn], a_in[:, ls],
                                      preferred_element_type=F32)
            for c in range(groups):
                if c == groups // 2:
                    ht_out[:, ls] = jnp.dot(u_ref[u_lo:u_lo + tn, :], xt_ref[:, ls],
                                            preferred_element_type=F32)
                _gate_block(ht_in, a_out, r2_ref, e2_ref, cnt_ref, e1_ref, first_key + c, c, lh)

    half_step(0, a_a, ht_b, a_b, ht_a, cnt_prev, e1_prev, groups)
    half_step(tn, a_b, ht_a, a_a, ht_b, cnt_cur, e1_cur, 0)

    @pl.when(p == pl.num_programs(1) - 1)
    def _():
        y = alpha * x_ref[...] + acc_scr[...].T
        o_ref[...] = _layer_norm_rows(y, g_ref[...], b_ref[...])


def _peer(u_b, x_t, v_t, r2, e2, cnt, e1, x, g, b, alpha, tt, tn):
    n, d = u_b.shape
    t = x.shape[0]
    n_pairs = n // (2 * tn)
    groups = tn // PEER_NKEYS
    cnt3 = cnt.reshape(PEER_HEADS, PEER_NKEYS, t)
    e13 = e1.reshape(PEER_HEADS, PEER_NKEYS, t)
    this_pair = lambda p: jnp.minimum(p, n_pairs - 1)
    last_pair = lambda p: jnp.maximum(p - 1, 0)
    tok = pl.BlockSpec((PEER_HEADS * PEER_NKEYS, tt), lambda j, p: (0, j))
    keys = lambda which: pl.BlockSpec((PEER_HEADS, 2 * groups, tt), lambda j, p: (0, which(p), j))
    vec = pl.BlockSpec((1, d), lambda j, p: (0, 0))
    return pl.pallas_call(
        functools.partial(_peer_kernel, alpha=alpha),
        grid=(t // tt, n_pairs + 1),
        in_specs=[pl.BlockSpec((2 * tn, d), lambda j, p: (this_pair(p), 0)),
                  pl.BlockSpec((d, tt), lambda j, p: (0, j)),
                  pl.BlockSpec((d, 2 * tn), lambda j, p: (0, last_pair(p))),
                  tok, tok, keys(last_pair), keys(this_pair), keys(last_pair), keys(this_pair),
                  pl.BlockSpec((tt, d), lambda j, p: (j, 0)), vec, vec],
        out_specs=pl.BlockSpec((tt, d), lambda j, p: (j, 0)),
        out_shape=jax.ShapeDtypeStruct((t, d), F32),
        scratch_shapes=[pltpu.VMEM((d, tt), F32),
                        pltpu.VMEM((tn, tt), F32), pltpu.VMEM((tn, tt), F32),
                        pltpu.VMEM((tn, tt), BF16), pltpu.VMEM((tn, tt), BF16)],
        compiler_params=_params("parallel", "arbitrary"),
        name="peer_experts",
    )(u_b, x_t, v_t, r2, e2, cnt3, cnt3, e13, e13, x, g.reshape(1, d), b.reshape(1, d))


def _rotate_half_columns(w):
    d, n = w.shape
    wh = w.reshape(d, n // HEAD_DIM, 2, HEAD_DIM // 2)
    return jnp.stack([-wh[:, :, 1], wh[:, :, 0]], axis=2).reshape(d, n)


def _rope_tables(seq):
    pos = jnp.arange(seq, dtype=F32)
    inv = jnp.power(ROPE_THETA, -jnp.arange(0, HEAD_DIM, 2, dtype=F32) / HEAD_DIM)
    ang = pos[:, None] * inv[None, :]
    cos = jnp.tile(jnp.cos(ang), (1, 2 * ATTN_HEADS))
    sin = jnp.tile(jnp.sin(ang), (1, 2 * ATTN_HEADS))
    return cos, sin


def kernel(x, w_in, conv_w, conv_b, conv_ln_g, conv_ln_b, w_out, ln1_g, ln1_b,
           peer_wq, peer_keys, peer_u, peer_v, ln2_g, ln2_b):
    batch, seq, d = x.shape
    depth = w_in.shape[0]
    t = batch * seq
    aw = ATTN_WIDTH
    alpha = (2 * depth) ** 0.25
    nb = seq // MOBA_BLOCK

    cos, sin = _rope_tables(seq)
    eye = (jnp.arange(seq + MOBA_BLOCK)[:, None] // MOBA_BLOCK
           == jnp.arange(LANES)[None, :]).astype(BF16)

    xf = x.reshape(t, d)
    for l in range(depth):
        wq, wk, rest = w_in[l][:, :aw], w_in[l][:, aw:2 * aw], w_in[l][:, 2 * aw:]
        w_ext = jnp.concatenate(
            [wq, _rotate_half_columns(wq), wk, _rotate_half_columns(wk), rest], axis=1).astype(BF16)
        proj = _matmul(xf, w_ext, tm=256, tn=w_ext.shape[1], out_dtype=F32, name="in_proj")
        qt, kb, vt, kmean = _rope(proj, cos, sin, seq)
        attn = _moba(qt, kb, vt, kmean.reshape(batch, nb, aw), eye, batch, seq)
        conv = _conv(proj, 5, 6, conv_w[l], conv_b[l], conv_ln_g[l], conv_ln_b[l],
                     batch, seq, ts=512)
        x1, x1t = _outproj(attn, conv, xf, w_out[l].astype(BF16), ln1_g[l], ln1_b[l], alpha, tm=256)

        groups = PEER_HEADS * 2
        ws_t = _fold_keys(peer_keys[l].reshape(groups, PEER_NKEYS, PEER_HALF), peer_wq[l])
        scores_t = _matmul(ws_t, x1t, tm=ws_t.shape[0], tn=512, out_dtype=F32, name="peer_scores")
        r2, e2, cnt, e1 = _route(scores_t, tl=256)
        xf = _peer(peer_u[l].astype(BF16), x1t, peer_v[l].astype(BF16).T,
                   r2, e2, cnt, e1, x1, ln2_g[l], ln2_b[l], alpha, tt=512, tn=512)
    return xf.reshape(batch, seq, d)
```

```python
import functools

import jax
import jax.numpy as jnp
from jax import lax
from jax.experimental import pallas as pl
from jax.experimental.pallas import tpu as pltpu

ATTN_HEADS = 8
HEAD_DIM = 64
ATTN_WIDTH = ATTN_HEADS * HEAD_DIM
CONV_KERNEL = 31
ROPE_THETA = 10000.0
MOBA_BLOCK = 256
MOBA_TOPK = 3
PEER_HEADS = 8
PEER_NKEYS = 128
PEER_HALF = 128
PEER_TOPK = 16
LN_EPS = 1e-5

LANES = 128
SUBLANES = 8
VMEM_LIMIT_BYTES = 56 * 1024 * 1024

NEG_BIG = -1e30
CONV_HALO = 32

F32 = jnp.float32
BF16 = jnp.bfloat16


def _params(*semantics):
    return pltpu.CompilerParams(dimension_semantics=semantics,
                                vmem_limit_bytes=VMEM_LIMIT_BYTES)


def _layer_norm_rows(y, g, b):
    mu = jnp.mean(y, axis=-1, keepdims=True)
    yc = y - mu
    var = jnp.mean(yc * yc, axis=-1, keepdims=True)
    return yc * lax.rsqrt(var + LN_EPS) * g + b


def _matmul_kernel(a_ref, b_ref, o_ref):
    a = a_ref[...].astype(BF16)
    o_ref[...] = jnp.dot(a, b_ref[...], preferred_element_type=F32).astype(o_ref.dtype)


def _matmul(a, b, *, tm, tn, out_dtype, name):
    m, k = a.shape
    _, n = b.shape
    return pl.pallas_call(
        _matmul_kernel,
        grid=(m // tm, n // tn),
        in_specs=[pl.BlockSpec((tm, k), lambda i, j: (i, 0)),
                  pl.BlockSpec((k, tn), lambda i, j: (0, j))],
        out_specs=pl.BlockSpec((tm, tn), lambda i, j: (i, j)),
        out_shape=jax.ShapeDtypeStruct((m, n), out_dtype),
        compiler_params=_params("parallel", "parallel"),
        name=name,
    )(a, b)


def _inproj_kernel(x_ref, w_ref, cos_ref, sin_ref, qt_ref, ko_ref, vt_ref, km_ref, ag_ref):
    w = ATTN_WIDTH
    proj = jnp.dot(x_ref[...].astype(BF16), w_ref[...], preferred_element_type=F32)
    cos = cos_ref[...]
    sin = sin_ref[...]
    qt_ref[...] = (proj[:, 0:w] * cos + proj[:, w:2 * w] * sin).T
    k = proj[:, 2 * w:3 * w] * cos + proj[:, 3 * w:4 * w] * sin
    ko_ref[...] = k.astype(BF16)
    vt_ref[...] = proj[:, 4 * w:5 * w].T.astype(BF16)
    km_ref[0] = jnp.mean(k, axis=0, keepdims=True)
    ag_ref[...] = proj[:, 5 * w:]


def _inproj(x, w_ext, cos, sin, seq):
    t, d = x.shape
    w = ATTN_WIDTH
    n = w_ext.shape[1]
    nblk = t // MOBA_BLOCK
    per_seq = seq // MOBA_BLOCK
    tab = pl.BlockSpec((MOBA_BLOCK, w), lambda i: (i % per_seq, 0))
    colmajor = pl.BlockSpec((w, MOBA_BLOCK), lambda i: (0, i))
    return pl.pallas_call(
        _inproj_kernel,
        grid=(nblk,),
        in_specs=[pl.BlockSpec((MOBA_BLOCK, d), lambda i: (i, 0)),
                  pl.BlockSpec((d, n), lambda i: (0, 0)), tab, tab],
        out_specs=[colmajor, pl.BlockSpec((MOBA_BLOCK, w), lambda i: (i, 0)), colmajor,
                   pl.BlockSpec((1, 1, w), lambda i: (i, 0, 0)),
                   pl.BlockSpec((MOBA_BLOCK, n - 5 * w), lambda i: (i, 0))],
        out_shape=[jax.ShapeDtypeStruct((w, t), F32),
                   jax.ShapeDtypeStruct((t, w), BF16),
                   jax.ShapeDtypeStruct((w, t), BF16),
                   jax.ShapeDtypeStruct((nblk, 1, w), F32),
                   jax.ShapeDtypeStruct((t, n - 5 * w), F32)],
        compiler_params=_params("parallel"),
        name="in_proj_rope",
    )(x, w_ext, cos, sin)


def _tree_reduce(op, s):
    r = s.shape[0]
    while r > SUBLANES:
        r //= 2
        s = op(s[0:r, :], s[r:2 * r, :])
    if op is jnp.add:
        return jnp.sum(s, axis=0, keepdims=True)
    return jnp.max(s, axis=0, keepdims=True)


def _moba_kernel(qt_ref, k_ref, vt_ref, km_ref, e_ref, o_ref, acc0, acc1, sa0, sa1, sb0, sb1,
                 *, scale, n_sel):
    blk = MOBA_BLOCK
    nb = km_ref.shape[1]
    seq = k_ref.shape[0]
    i = pl.program_id(2)
    qt = qt_ref[...]
    km = km_ref[0]
    chan = lax.broadcasted_iota(jnp.int32, (LANES, blk), 0)
    km_chan = lax.broadcasted_iota(jnp.int32, (nb, LANES), 1)
    blk_ix = lax.broadcasted_iota(jnp.int32, (nb, blk), 0)
    pad_ix = lax.broadcasted_iota(jnp.int32, (LANES - nb, blk), 0)
    accs = (acc0, acc1)
    slots = ((sa0, sa1), (sb0, sb1))

    qexts = []
    for h in range(2):
        kmh = jnp.where(km_chan // HEAD_DIM == h, km, 0.0)
        gate = jnp.dot(kmh, qt, precision=lax.Precision.HIGHEST,
                       preferred_element_type=F32)
        g = jnp.where(blk_ix < i, gate, -jnp.inf)
        bias = jnp.full((nb, blk), NEG_BIG, F32)
        for _ in range(n_sel):
            mx = jnp.max(g, axis=0, keepdims=True)
            first = jnp.min(jnp.where(g == mx, blk_ix, nb), axis=0, keepdims=True)
            first = jnp.where(mx > -jnp.inf, first, nb)
            pick = blk_ix == first
            bias = jnp.where(pick, 0.0, bias)
            g = jnp.where(pick, -jnp.inf, g)
        qh = (jnp.where(chan // HEAD_DIM == h, qt, 0.0) * scale).astype(BF16)
        pad = jnp.where(pad_ix == 0, NEG_BIG, 0.0).astype(BF16)
        qexts.append(jnp.concatenate([qh, bias.astype(BF16), pad], axis=0))

    own = pl.multiple_of(i * blk, blk)
    kd = k_ref[pl.ds(own, blk), :]
    vd = vt_ref[:, pl.ds(own, blk)]
    key_ix = lax.broadcasted_iota(jnp.int32, (blk, blk), 0)
    qry_ix = lax.broadcasted_iota(jnp.int32, (blk, blk), 1)
    stats = []
    ss = [jnp.dot(kd, qexts[h][0:LANES, :], preferred_element_type=F32) for h in range(2)]
    for h in range(2):
        s = jnp.where(key_ix <= qry_ix, ss[h], NEG_BIG)
        m = _tree_reduce(jnp.maximum, s)
        p = jnp.exp(s - m)
        stats += [m, _tree_reduce(jnp.add, p)]
        accs[h][...] = jnp.dot(vd, p.astype(BF16), preferred_element_type=F32)

    def key_offset(j):
        return pl.multiple_of(jnp.where(j < i, j, 0) * blk, blk)

    def scores(j):
        e_off = pl.multiple_of(jnp.where(j < i, j * blk, seq), blk)
        kext = jnp.concatenate([k_ref[pl.ds(key_offset(j), blk), :], e_ref[pl.ds(e_off, blk), :]],
                               axis=1)
        return [jnp.dot(kext, qexts[h], preferred_element_type=F32) for h in range(2)]

    def consume(slot, j, carry):
        vj = vt_ref[:, pl.ds(key_offset(j), blk)]
        out = []
        for h in range(2):
            m_prev, l_prev = carry[2 * h], carry[2 * h + 1]
            s = slot[h][...]
            m_new = jnp.maximum(m_prev, _tree_reduce(jnp.maximum, s))
            alpha = jnp.exp(m_prev - m_new)
            p = jnp.exp(s - m_new)
            accs[h][...] = alpha * accs[h][...] + jnp.dot(vj, p.astype(BF16),
                                                          preferred_element_type=F32)
            out += [m_new, alpha * l_prev + _tree_reduce(jnp.add, p)]
        return tuple(out)

    def fill(slot, ss):
        for h in range(2):
            slot[h][...] = ss[h]

    fill(slots[0], scores(0))

    def body(jj, carry):
        j0 = 2 * jj
        nxt = scores(j0 + 1)
        carry = consume(slots[0], j0, carry)
        fill(slots[1], nxt)
        nxt = scores(j0 + 2)
        carry = consume(slots[1], j0 + 1, carry)
        fill(slots[0], nxt)
        return carry

    stats = lax.fori_loop(0, (i + 1) // 2, body, tuple(stats))
    o0 = acc0[0:HEAD_DIM, :] / stats[1]
    o1 = acc1[HEAD_DIM:, :] / stats[3]
    o_ref[...] = jnp.concatenate([o0, o1], axis=0).T.astype(o_ref.dtype)


def _moba(qt, kb, vt, kmean, eye, batch, seq):
    t = kb.shape[0]
    blk = MOBA_BLOCK
    nb = seq // blk
    n_sel = max(1, min(MOBA_TOPK, nb - 1))
    pairs = ATTN_WIDTH // LANES
    kernel = functools.partial(_moba_kernel, scale=HEAD_DIM ** -0.5, n_sel=n_sel)
    score_buf = pltpu.VMEM((blk, blk), F32)
    return pl.pallas_call(
        kernel,
        grid=(batch, pairs, nb),
        in_specs=[pl.BlockSpec((LANES, blk), lambda b, p, i: (p, b * nb + i)),
                  pl.BlockSpec((seq, LANES), lambda b, p, i: (b, p)),
                  pl.BlockSpec((LANES, seq), lambda b, p, i: (p, b)),
                  pl.BlockSpec((1, nb, LANES), lambda b, p, i: (b, 0, p)),
                  pl.BlockSpec((seq + blk, LANES), lambda b, p, i: (0, 0))],
        out_specs=pl.BlockSpec((blk, LANES), lambda b, p, i: (b * nb + i, p)),
        out_shape=jax.ShapeDtypeStruct((t, ATTN_WIDTH), BF16),
        scratch_shapes=[pltpu.VMEM((LANES, blk), F32), pltpu.VMEM((LANES, blk), F32),
                        score_buf, score_buf, score_buf, score_buf],
        compiler_params=_params("parallel", "parallel", "arbitrary"),
        name="moba_attention",
    )(qt, kb, vt, kmean, eye)


def _conv_kernel(a_ref, g_ref, ah_ref, gh_ref, w_ref, b_ref, lg_ref, lb_ref, o_ref, u_scr,
                 *, ts):
    j = pl.program_id(1)
    halo = ah_ref[...] * jax.nn.sigmoid(gh_ref[...])
    u_scr[0, 0:CONV_HALO, :] = jnp.where(j > 0, halo, 0.0)
    u_scr[0, CONV_HALO:, :] = a_ref[...] * jax.nn.sigmoid(g_ref[...])
    rows = CONV_HALO + ts
    for r in range(1, SUBLANES):
        u_scr[r, 0:rows - SUBLANES, :] = u_scr[0, r:r + rows - SUBLANES, :]
    first = CONV_HALO - (CONV_KERNEL - 1)
    acc = jnp.zeros((ts, a_ref.shape[1]), F32)
    for tap in range(CONV_KERNEL):
        r = (first + tap) % SUBLANES
        base = first + tap - r
        acc = acc + w_ref[tap:tap + 1, :] * u_scr[r, base:base + ts, :]
    y = _layer_norm_rows(acc + b_ref[...], lg_ref[...], lb_ref[...])
    o_ref[...] = (y * jax.nn.sigmoid(y)).astype(o_ref.dtype)


def _conv(proj, a_col, g_col, w, b, ln_g, ln_b, batch, seq, ts):
    t = proj.shape[0]
    cw = w.shape[1]
    per_seq = seq // ts
    halo_per_tile = ts // CONV_HALO
    cur = lambda c: pl.BlockSpec((ts, cw), lambda bi, j, c=c: (bi * per_seq + j, c))
    halo = lambda c: pl.BlockSpec(
        (CONV_HALO, cw),
        lambda bi, j, c=c: (jnp.maximum((bi * per_seq + j) * halo_per_tile - 1, 0), c))
    vec = pl.BlockSpec((1, cw), lambda bi, j: (0, 0))
    return pl.pallas_call(
        functools.partial(_conv_kernel, ts=ts),
        grid=(batch, per_seq),
        in_specs=[cur(a_col), cur(g_col), halo(a_col), halo(g_col),
                  pl.BlockSpec((CONV_KERNEL, cw), lambda bi, j: (0, 0)), vec, vec, vec],
        out_specs=pl.BlockSpec((ts, cw), lambda bi, j: (bi * per_seq + j, 0)),
        out_shape=jax.ShapeDtypeStruct((t, cw), BF16),
        scratch_shapes=[pltpu.VMEM((SUBLANES, CONV_HALO + ts, cw), F32)],
        compiler_params=_params("parallel", "parallel"),
        name="conformer_conv",
    )(proj, proj, proj, proj, w, b.reshape(1, cw), ln_g.reshape(1, cw), ln_b.reshape(1, cw))


def _outproj_kernel(at_ref, cv_ref, x_ref, w_ref, g_ref, b_ref, o_ref, ot_ref, *, alpha):
    aw = at_ref.shape[1]
    mixed = jnp.dot(at_ref[...], w_ref[0:aw, :], preferred_element_type=F32)
    mixed = mixed + jnp.dot(cv_ref[...], w_ref[aw:, :], preferred_element_type=F32)
    y = _layer_norm_rows(alpha * x_ref[...] + mixed, g_ref[...], b_ref[...])
    o_ref[...] = y
    ot_ref[...] = y.T.astype(BF16)


def _outproj(attn, conv, x, w_out, g, b, alpha, tm):
    t, d = x.shape
    aw = attn.shape[1]
    cw = conv.shape[1]
    vec = pl.BlockSpec((1, d), lambda i: (0, 0))
    return pl.pallas_call(
        functools.partial(_outproj_kernel, alpha=alpha),
        grid=(t // tm,),
        in_specs=[pl.BlockSpec((tm, aw), lambda i: (i, 0)),
                  pl.BlockSpec((tm, cw), lambda i: (i, 0)),
                  pl.BlockSpec((tm, d), lambda i: (i, 0)),
                  pl.BlockSpec((aw + cw, d), lambda i: (0, 0)), vec, vec],
        out_specs=[pl.BlockSpec((tm, d), lambda i: (i, 0)),
                   pl.BlockSpec((d, tm), lambda i: (0, i))],
        out_shape=[jax.ShapeDtypeStruct((t, d), F32),
                   jax.ShapeDtypeStruct((d, t), BF16)],
        compiler_params=_params("parallel"),
        name="outproj_ln",
    )(attn, conv, x, w_out, g.reshape(1, d), b.reshape(1, d))


def _fold_kernel(k_ref, w_ref, o_ref):
    o_ref[...] = lax.dot_general(k_ref[0], w_ref[...], (((1,), (1,)), ((), ())),
                                 precision=lax.Precision.HIGHEST,
                                 preferred_element_type=F32).astype(o_ref.dtype)


def _fold_keys(keys, wq):
    d = wq.shape[0]
    groups = keys.shape[0]
    return pl.pallas_call(
        _fold_kernel,
        grid=(groups,),
        in_specs=[pl.BlockSpec((1, PEER_NKEYS, PEER_HALF), lambda i: (i, 0, 0)),
                  pl.BlockSpec((d, PEER_HALF), lambda i: (0, i))],
        out_specs=pl.BlockSpec((PEER_NKEYS, d), lambda i: (i, 0)),
        out_shape=jax.ShapeDtypeStruct((groups * PEER_NKEYS, d), BF16),
        compiler_params=_params("parallel"),
        name="peer_fold_keys",
    )(keys, wq)


assert PEER_TOPK == 2 * SUBLANES


def _top_values_and_ranks(s, k, want_rank):
    rank = jnp.full(s.shape, float(k), F32) if want_rank else None
    cur = s
    tops = []
    for r in range(k):
        mx = jnp.max(cur, axis=0, keepdims=True)
        hit = cur == mx
        if want_rank:
            rank = jnp.where(hit, float(r), rank)
        cur = jnp.where(hit, -jnp.inf, cur)
        tops.append(mx)
    return jnp.concatenate(tops, axis=0), rank


def _route_kernel(s_ref, r2_ref, e2_ref, cnt_ref, e1_ref):
    nk = PEER_NKEYS
    k = PEER_TOPK
    sub = SUBLANES
    n = s_ref.shape[1]
    row = lax.broadcasted_iota(jnp.int32, (sub, n), 0)

    def head(h, carry):
        base = pl.multiple_of(h * 2 * nk, 2 * nk)
        s1 = s_ref[pl.ds(base, nk), :]
        s2 = s_ref[pl.ds(base + nk, nk), :]
        a, _ = _top_values_and_ranks(s1, k, False)
        b, rank2 = _top_values_and_ranks(s2, k, True)
        cands = [a + b[0:1, :]]
        for jj in range(1, sub):
            keep = k // (jj + 1)
            v = a[0:sub, :] + b[jj:jj + 1, :]
            cands.append(v if keep >= sub else jnp.where(row < keep, v, -jnp.inf))
        cands.append(a[0:1, :] + b[sub:k, :])
        z = jnp.concatenate(cands, axis=0)
        cur = z
        for _ in range(k):
            mx = jnp.max(cur, axis=0, keepdims=True)
            cur = jnp.where(cur == mx, -jnp.inf, cur)
        chosen = cur != z
        zmax = a[0:1, :] + b[0:1, :]
        denom = jnp.sum(jnp.where(chosen, jnp.exp(z - zmax), 0.0), axis=0, keepdims=True)
        ones = jnp.where(chosen, 1.0, 0.0)
        low = ones[0:sub, :]
        for jj in range(1, sub):
            low = low + ones[k + (jj - 1) * sub:k + jj * sub, :]
        tail = jnp.sum(ones[k + (sub - 1) * sub:, :], axis=0, keepdims=True)
        low = low + jnp.where(row == 0, tail, 0.0)
        cnt_sorted = jnp.concatenate([low, ones[sub:k, :]], axis=0)
        cnt = jnp.zeros(s1.shape, F32)
        for i in range(k):
            cnt = jnp.where(s1 == a[i:i + 1, :], cnt_sorted[i:i + 1, :], cnt)
        out = pl.multiple_of(h * nk, nk)
        r2_ref[pl.ds(out, nk), :] = rank2.astype(BF16)
        e2_ref[pl.ds(out, nk), :] = jnp.exp(s2 - b[0:1, :]).astype(BF16)
        cnt_ref[pl.ds(out, nk), :] = cnt
        e1_ref[pl.ds(out, nk), :] = jnp.exp(s1 - a[0:1, :]) * (1.0 / denom)
        return carry

    lax.fori_loop(0, PEER_HEADS, head, 0)


def _route(scores_t, tl):
    rows, t = scores_t.shape
    out_rows = rows // 2
    spec = pl.BlockSpec((out_rows, tl), lambda i: (0, i))
    shape = lambda dtype: jax.ShapeDtypeStruct((out_rows, t), dtype)
    return pl.pallas_call(
        _route_kernel,
        grid=(t // tl,),
        in_specs=[pl.BlockSpec((rows, tl), lambda i: (0, i))],
        out_specs=[spec, spec, spec, spec],
        out_shape=[shape(BF16), shape(BF16), shape(F32), shape(F32)],
        compiler_params=_params("parallel"),
        name="peer_route",
    )(scores_t)


_GATE_ROWS = 16
_GATE_LANES = 256


def _gate_block(ht_scr, a_scr, r2_ref, e2_ref, cnt_ref, e1_ref, key, c, lh):
    nk = PEER_NKEYS
    shape = (_GATE_ROWS, _GATE_LANES)
    ls = pl.ds(lh * _GATE_LANES, _GATE_LANES)
    cnt_b = [jnp.broadcast_to(cnt_ref[h, key:key + 1, ls], shape).astype(BF16)
             for h in range(PEER_HEADS)]
    e1_b = [jnp.broadcast_to(e1_ref[h, key:key + 1, ls], shape).astype(BF16)
            for h in range(PEER_HEADS)]
    for r in range(nk // _GATE_ROWS):
        gate = None
        for h in range(PEER_HEADS):
            rs = pl.ds(h * nk + r * _GATE_ROWS, _GATE_ROWS)
            e2 = e2_ref[rs, ls]
            term = jnp.where(r2_ref[rs, ls] < cnt_b[h], e2, jnp.zeros_like(e2)) * e1_b[h]
            gate = term if gate is None else gate + term
        hs = pl.ds(c * nk + r * _GATE_ROWS, _GATE_ROWS)
        hv = ht_scr[hs, ls].astype(BF16)
        half = hv * 0.5
        a_scr[hs, ls] = gate * (half + half * lax.erf(hv * (2.0 ** -0.5)))


def _peer_kernel(u_ref, xt_ref, vt_ref, r2_ref, e2_ref, cnt_prev, cnt_cur, e1_prev, e1_cur,
                 x_ref, g_ref, b_ref, o_ref, acc_scr, ht_a, ht_b, a_a, a_b, *, alpha):
    tn, tt = ht_a.shape
    groups = tn // PEER_NKEYS
    p = pl.program_id(1)

    @pl.when(p == 0)
    def _():
        acc_scr[...] = jnp.zeros_like(acc_scr)
        ht_b[...] = jnp.zeros_like(ht_b)
        a_a[...] = jnp.zeros_like(a_a)

    def half_step(u_lo, a_in, ht_in, a_out, ht_out, cnt_ref, e1_ref, first_key):
        for lh in range(tt // _GATE_LANES):
            ls = pl.ds(lh * _GATE_LANES, _GATE_LANES)
            acc_scr[:, ls] += jnp.dot(vt_ref[:, u_lo:u_lo + tn], a_in[:, ls],
                                      preferred_element_type=F32)
            for c in range(groups):
                if c == groups // 2:
                    ht_out[:, ls] = jnp.dot(u_ref[u_lo:u_lo + tn, :], xt_ref[:, ls],
                                            preferred_element_type=F32)
                _gate_block(ht_in, a_out, r2_ref, e2_ref, cnt_ref, e1_ref, first_key + c, c, lh)

    half_step(0, a_a, ht_b, a_b, ht_a, cnt_prev, e1_prev, groups)
    half_step(tn, a_b, ht_a, a_a, ht_b, cnt_cur, e1_cur, 0)

    @pl.when(p == pl.num_programs(1) - 1)
    def _():
        y = alpha * x_ref[...] + acc_scr[...].T
        o_ref[...] = _layer_norm_rows(y, g_ref[...], b_ref[...])


def _peer(u_b, x_t, v_t, r2, e2, cnt, e1, x, g, b, alpha, tt, tn):
    n, d = u_b.shape
    t = x.shape[0]
    n_pairs = n // (2 * tn)
    groups = tn // PEER_NKEYS
    cnt3 = cnt.reshape(PEER_HEADS, PEER_NKEYS, t)
    e13 = e1.reshape(PEER_HEADS, PEER_NKEYS, t)
    this_pair = lambda p: jnp.minimum(p, n_pairs - 1)
    last_pair = lambda p: jnp.maximum(p - 1, 0)
    tok = pl.BlockSpec((PEER_HEADS * PEER_NKEYS, tt), lambda j, p: (0, j))
    keys = lambda which: pl.BlockSpec((PEER_HEADS, 2 * groups, tt), lambda j, p: (0, which(p), j))
    vec = pl.BlockSpec((1, d), lambda j, p: (0, 0))
    return pl.pallas_call(
        functools.partial(_peer_kernel, alpha=alpha),
        grid=(t // tt, n_pairs + 1),
        in_specs=[pl.BlockSpec((2 * tn, d), lambda j, p: (this_pair(p), 0)),
                  pl.BlockSpec((d, tt), lambda j, p: (0, j)),
                  pl.BlockSpec((d, 2 * tn), lambda j, p: (0, last_pair(p))),
                  tok, tok, keys(last_pair), keys(this_pair), keys(last_pair), keys(this_pair),
                  pl.BlockSpec((tt, d), lambda j, p: (j, 0)), vec, vec],
        out_specs=pl.BlockSpec((tt, d), lambda j, p: (j, 0)),
        out_shape=jax.ShapeDtypeStruct((t, d), F32),
        scratch_shapes=[pltpu.VMEM((d, tt), F32),
                        pltpu.VMEM((tn, tt), F32), pltpu.VMEM((tn, tt), F32),
                        pltpu.VMEM((tn, tt), BF16), pltpu.VMEM((tn, tt), BF16)],
        compiler_params=_params("parallel", "arbitrary"),
        name="peer_experts",
    )(u_b, x_t, v_t, r2, e2, cnt3, cnt3, e13, e13, x, g.reshape(1, d), b.reshape(1, d))


def _rotate_half_columns(w):
    d, n = w.shape
    wh = w.reshape(d, n // HEAD_DIM, 2, HEAD_DIM // 2)
    return jnp.stack([-wh[:, :, 1], wh[:, :, 0]], axis=2).reshape(d, n)


def _rope_tables(seq):
    pos = jnp.arange(seq, dtype=F32)
    inv = jnp.power(ROPE_THETA, -jnp.arange(0, HEAD_DIM, 2, dtype=F32) / HEAD_DIM)
    ang = pos[:, None] * inv[None, :]
    cos = jnp.tile(jnp.cos(ang), (1, 2 * ATTN_HEADS))
    sin = jnp.tile(jnp.sin(ang), (1, 2 * ATTN_HEADS))
    return cos, sin


def kernel(x, w_in, conv_w, conv_b, conv_ln_g, conv_ln_b, w_out, ln1_g, ln1_b,
           peer_wq, peer_keys, peer_u, peer_v, ln2_g, ln2_b):
    batch, seq, d = x.shape
    depth = w_in.shape[0]
    t = batch * seq
    aw = ATTN_WIDTH
    alpha = (2 * depth) ** 0.25
    nb = seq // MOBA_BLOCK

    cos, sin = _rope_tables(seq)
    eye = (jnp.arange(seq + MOBA_BLOCK)[:, None] // MOBA_BLOCK
           == jnp.arange(LANES)[None, :]).astype(BF16)

    xf = x.reshape(t, d)
    for l in range(depth):
        wq, wk, rest = w_in[l][:, :aw], w_in[l][:, aw:2 * aw], w_in[l][:, 2 * aw:]
        w_ext = jnp.concatenate(
            [wq, _rotate_half_columns(wq), wk, _rotate_half_columns(wk), rest], axis=1).astype(BF16)
        qt, kb, vt, kmean, ag = _inproj(xf, w_ext, cos, sin, seq)
        attn = _moba(qt, kb, vt, kmean.reshape(batch, nb, aw), eye, batch, seq)
        conv = _conv(ag, 0, 1, conv_w[l], conv_b[l], conv_ln_g[l], conv_ln_b[l],
                     batch, seq, ts=512)
        x1, x1t = _outproj(attn, conv, xf, w_out[l].astype(BF16), ln1_g[l], ln1_b[l], alpha, tm=256)

        groups = PEER_HEADS * 2
        ws_t = _fold_keys(peer_keys[l].reshape(groups, PEER_NKEYS, PEER_HALF), peer_wq[l])
        scores_t = _matmul(ws_t, x1t, tm=ws_t.shape[0], tn=512, out_dtype=F32, name="peer_scores")
        r2, e2, cnt, e1 = _route(scores_t, tl=256)
        xf = _peer(peer_u[l].astype(BF16), x1t, peer_v[l].astype(BF16).T,
                   r2, e2, cnt, e1, x1, ln2_g[l], ln2_b[l], alpha, tt=512, tn=512)
    return xf.reshape(batch, seq, d)
```

```python
import functools

import jax
import jax.numpy as jnp
from jax import lax
from jax.experimental import pallas as pl
from jax.experimental.pallas import tpu as pltpu

ATTN_HEADS = 8
HEAD_DIM = 64
ATTN_WIDTH = ATTN_HEADS * HEAD_DIM
CONV_KERNEL = 31
ROPE_THETA = 10000.0
MOBA_BLOCK = 256
MOBA_TOPK = 3
PEER_HEADS = 8
PEER_NKEYS = 128
PEER_HALF = 128
PEER_TOPK = 16
LN_EPS = 1e-5

LANES = 128
SUBLANES = 8
VMEM_LIMIT_BYTES = 56 * 1024 * 1024

NEG_BIG = -1e30
CONV_HALO = 32

F32 = jnp.float32
BF16 = jnp.bfloat16


def _params(*semantics):
    return pltpu.CompilerParams(dimension_semantics=semantics,
                                vmem_limit_bytes=VMEM_LIMIT_BYTES)


def _layer_norm_rows(y, g, b):
    mu = jnp.mean(y, axis=-1, keepdims=True)
    yc = y - mu
    var = jnp.mean(yc * yc, axis=-1, keepdims=True)
    return yc * lax.rsqrt(var + LN_EPS) * g + b


def _matmul_kernel(a_ref, b_ref, o_ref):
    a = a_ref[...].astype(BF16)
    o_ref[...] = jnp.dot(a, b_ref[...], preferred_element_type=F32).astype(o_ref.dtype)


def _matmul(a, b, *, tm, tn, out_dtype, name):
    m, k = a.shape
    _, n = b.shape
    return pl.pallas_call(
        _matmul_kernel,
        grid=(m // tm, n // tn),
        in_specs=[pl.BlockSpec((tm, k), lambda i, j: (i, 0)),
                  pl.BlockSpec((k, tn), lambda i, j: (0, j))],
        out_specs=pl.BlockSpec((tm, tn), lambda i, j: (i, j)),
        out_shape=jax.ShapeDtypeStruct((m, n), out_dtype),
        compiler_params=_params("parallel", "parallel"),
        name=name,
    )(a, b)


def _rotate_half(x):
    n = x.shape[1]
    half = HEAD_DIM // 2
    col = lax.broadcasted_iota(jnp.int32, x.shape, 1)
    from_above = pltpu.roll(x, n - half, 1)
    from_below = pltpu.roll(x, half, 1)
    return jnp.where(col % HEAD_DIM < half, -from_above, from_below)


def _inproj_kernel(x_ref, w_ref, cos_ref, sin_ref, qt_ref, ko_ref, vt_ref, km_ref, ag_ref):
    w = ATTN_WIDTH
    proj = jnp.dot(x_ref[...].astype(BF16), w_ref[...], preferred_element_type=F32)
    cos = cos_ref[...]
    sin = sin_ref[...]
    q = proj[:, 0:w]
    k = proj[:, w:2 * w]
    qt_ref[...] = (q * cos + _rotate_half(q) * sin).T
    k = k * cos + _rotate_half(k) * sin
    ko_ref[...] = k.astype(BF16)
    vt_ref[...] = proj[:, 2 * w:3 * w].T.astype(BF16)
    km_ref[0] = jnp.mean(k, axis=0, keepdims=True)
    ag_ref[...] = proj[:, 3 * w:]


def _inproj(x, w_ext, cos, sin, seq):
    t, d = x.shape
    w = ATTN_WIDTH
    n = w_ext.shape[1]
    nblk = t // MOBA_BLOCK
    per_seq = seq // MOBA_BLOCK
    tab = pl.BlockSpec((MOBA_BLOCK, w), lambda i: (i % per_seq, 0))
    colmajor = pl.BlockSpec((w, MOBA_BLOCK), lambda i: (0, i))
    return pl.pallas_call(
        _inproj_kernel,
        grid=(nblk,),
        in_specs=[pl.BlockSpec((MOBA_BLOCK, d), lambda i: (i, 0)),
                  pl.BlockSpec((d, n), lambda i: (0, 0)), tab, tab],
        out_specs=[colmajor, pl.BlockSpec((MOBA_BLOCK, w), lambda i: (i, 0)), colmajor,
                   pl.BlockSpec((1, 1, w), lambda i: (i, 0, 0)),
                   pl.BlockSpec((MOBA_BLOCK, n - 3 * w), lambda i: (i, 0))],
        out_shape=[jax.ShapeDtypeStruct((w, t), F32),
                   jax.ShapeDtypeStruct((t, w), BF16),
                   jax.ShapeDtypeStruct((w, t), BF16),
                   jax.ShapeDtypeStruct((nblk, 1, w), F32),
                   jax.ShapeDtypeStruct((t, n - 3 * w), F32)],
        compiler_params=_params("parallel"),
        name="in_proj_rope",
    )(x, w_ext, cos, sin)


def _tree_reduce(op, s):
    r = s.shape[0]
    while r > SUBLANES:
        r //= 2
        s = op(s[0:r, :], s[r:2 * r, :])
    if op is jnp.add:
        return jnp.sum(s, axis=0, keepdims=True)
    return jnp.max(s, axis=0, keepdims=True)


def _moba_kernel(qt_ref, k_ref, vt_ref, km_ref, e_ref, o_ref, acc0, acc1, sa0, sa1, sb0, sb1,
                 *, scale, n_sel):
    blk = MOBA_BLOCK
    nb = km_ref.shape[1]
    seq = k_ref.shape[0]
    i = pl.program_id(2)
    qt = qt_ref[...]
    km = km_ref[0]
    chan = lax.broadcasted_iota(jnp.int32, (LANES, blk), 0)
    km_chan = lax.broadcasted_iota(jnp.int32, (nb, LANES), 1)
    blk_ix = lax.broadcasted_iota(jnp.int32, (nb, blk), 0)
    pad_ix = lax.broadcasted_iota(jnp.int32, (LANES - nb, blk), 0)
    accs = (acc0, acc1)
    slots = ((sa0, sa1), (sb0, sb1))

    qexts = []
    for h in range(2):
        kmh = jnp.where(km_chan // HEAD_DIM == h, km, 0.0)
        gate = jnp.dot(kmh, qt, precision=lax.Precision.HIGHEST,
                       preferred_element_type=F32)
        g = jnp.where(blk_ix < i, gate, -jnp.inf)
        bias = jnp.full((nb, blk), NEG_BIG, F32)
        for _ in range(n_sel):
            mx = jnp.max(g, axis=0, keepdims=True)
            first = jnp.min(jnp.where(g == mx, blk_ix, nb), axis=0, keepdims=True)
            first = jnp.where(mx > -jnp.inf, first, nb)
            pick = blk_ix == first
            bias = jnp.where(pick, 0.0, bias)
            g = jnp.where(pick, -jnp.inf, g)
        qh = (jnp.where(chan // HEAD_DIM == h, qt, 0.0) * scale).astype(BF16)
        pad = jnp.where(pad_ix == 0, NEG_BIG, 0.0).astype(BF16)
        qexts.append(jnp.concatenate([qh, bias.astype(BF16), pad], axis=0))

    own = pl.multiple_of(i * blk, blk)
    kd = k_ref[pl.ds(own, blk), :]
    vd = vt_ref[:, pl.ds(own, blk)]
    key_ix = lax.broadcasted_iota(jnp.int32, (blk, blk), 0)
    qry_ix = lax.broadcasted_iota(jnp.int32, (blk, blk), 1)
    stats = []
    ss = [jnp.dot(kd, qexts[h][0:LANES, :], preferred_element_type=F32) for h in range(2)]
    for h in range(2):
        s = jnp.where(key_ix <= qry_ix, ss[h], NEG_BIG)
        m = _tree_reduce(jnp.maximum, s)
        p = jnp.exp(s - m)
        stats += [m, _tree_reduce(jnp.add, p)]
        accs[h][...] = jnp.dot(vd, p.astype(BF16), preferred_element_type=F32)

    def key_offset(j):
        return pl.multiple_of(jnp.where(j < i, j, 0) * blk, blk)

    def scores(j):
        e_off = pl.multiple_of(jnp.where(j < i, j * blk, seq), blk)
        kext = jnp.concatenate([k_ref[pl.ds(key_offset(j), blk), :], e_ref[pl.ds(e_off, blk), :]],
                               axis=1)
        return [jnp.dot(kext, qexts[h], preferred_element_type=F32) for h in range(2)]

    def consume(slot, j, carry):
        vj = vt_ref[:, pl.ds(key_offset(j), blk)]
        out = []
        for h in range(2):
            m_prev, l_prev = carry[2 * h], carry[2 * h + 1]
            s = slot[h][...]
            m_new = jnp.maximum(m_prev, _tree_reduce(jnp.maximum, s))
            alpha = jnp.exp(m_prev - m_new)
            p = jnp.exp(s - m_new)
            accs[h][...] = alpha * accs[h][...] + jnp.dot(vj, p.astype(BF16),
                                                          preferred_element_type=F32)
            out += [m_new, alpha * l_prev + _tree_reduce(jnp.add, p)]
        return tuple(out)

    def fill(slot, ss):
        for h in range(2):
            slot[h][...] = ss[h]

    fill(slots[0], scores(0))

    def body(jj, carry):
        j0 = 2 * jj
        nxt = scores(j0 + 1)
        carry = consume(slots[0], j0, carry)
        fill(slots[1], nxt)
        nxt = scores(j0 + 2)
        carry = consume(slots[1], j0 + 1, carry)
        fill(slots[0], nxt)
        return carry

    stats = lax.fori_loop(0, (i + 1) // 2, body, tuple(stats))
    o0 = acc0[0:HEAD_DIM, :] / stats[1]
    o1 = acc1[HEAD_DIM:, :] / stats[3]
    o_ref[...] = jnp.concatenate([o0, o1], axis=0).T.astype(o_ref.dtype)


def _moba(qt, kb, vt, kmean, eye, batch, seq):
    t = kb.shape[0]
    blk = MOBA_BLOCK
    nb = seq // blk
    n_sel = max(1, min(MOBA_TOPK, nb - 1))
    pairs = ATTN_WIDTH // LANES
    kernel = functools.partial(_moba_kernel, scale=HEAD_DIM ** -0.5, n_sel=n_sel)
    score_buf = pltpu.VMEM((blk, blk), F32)
    return pl.pallas_call(
        kernel,
        grid=(batch, pairs, nb),
        in_specs=[pl.BlockSpec((LANES, blk), lambda b, p, i: (p, b * nb + i)),
                  pl.BlockSpec((seq, LANES), lambda b, p, i: (b, p)),
                  pl.BlockSpec((LANES, seq), lambda b, p, i: (p, b)),
                  pl.BlockSpec((1, nb, LANES), lambda b, p, i: (b, 0, p)),
                  pl.BlockSpec((seq + blk, LANES), lambda b, p, i: (0, 0))],
        out_specs=pl.BlockSpec((blk, LANES), lambda b, p, i: (b * nb + i, p)),
        out_shape=jax.ShapeDtypeStruct((t, ATTN_WIDTH), BF16),
        scratch_shapes=[pltpu.VMEM((LANES, blk), F32), pltpu.VMEM((LANES, blk), F32),
                        score_buf, score_buf, score_buf, score_buf],
        compiler_params=_params("parallel", "parallel", "arbitrary"),
        name="moba_attention",
    )(qt, kb, vt, kmean, eye)


def _conv_kernel(a_ref, g_ref, ah_ref, gh_ref, w_ref, b_ref, lg_ref, lb_ref, o_ref, u_scr,
                 *, ts):
    j = pl.program_id(1)
    halo = ah_ref[...] * jax.nn.sigmoid(gh_ref[...])
    u_scr[0, 0:CONV_HALO, :] = jnp.where(j > 0, halo, 0.0)
    u_scr[0, CONV_HALO:, :] = a_ref[...] * jax.nn.sigmoid(g_ref[...])
    rows = CONV_HALO + ts
    for r in range(1, SUBLANES):
        u_scr[r, 0:rows - SUBLANES, :] = u_scr[0, r:r + rows - SUBLANES, :]
    first = CONV_HALO - (CONV_KERNEL - 1)
    acc = jnp.zeros((ts, a_ref.shape[1]), F32)
    for tap in range(CONV_KERNEL):
        r = (first + tap) % SUBLANES
        base = first + tap - r
        acc = acc + w_ref[tap:tap + 1, :] * u_scr[r, base:base + ts, :]
    y = _layer_norm_rows(acc + b_ref[...], lg_ref[...], lb_ref[...])
    o_ref[...] = (y * jax.nn.sigmoid(y)).astype(o_ref.dtype)


def _conv(proj, a_col, g_col, w, b, ln_g, ln_b, batch, seq, ts):
    t = proj.shape[0]
    cw = w.shape[1]
    per_seq = seq // ts
    halo_per_tile = ts // CONV_HALO
    cur = lambda c: pl.BlockSpec((ts, cw), lambda bi, j, c=c: (bi * per_seq + j, c))
    halo = lambda c: pl.BlockSpec(
        (CONV_HALO, cw),
        lambda bi, j, c=c: (jnp.maximum((bi * per_seq + j) * halo_per_tile - 1, 0), c))
    vec = pl.BlockSpec((1, cw), lambda bi, j: (0, 0))
    return pl.pallas_call(
        functools.partial(_conv_kernel, ts=ts),
        grid=(batch, per_seq),
        in_specs=[cur(a_col), cur(g_col), halo(a_col), halo(g_col),
                  pl.BlockSpec((CONV_KERNEL, cw), lambda bi, j: (0, 0)), vec, vec, vec],
        out_specs=pl.BlockSpec((ts, cw), lambda bi, j: (bi * per_seq + j, 0)),
        out_shape=jax.ShapeDtypeStruct((t, cw), BF16),
        scratch_shapes=[pltpu.VMEM((SUBLANES, CONV_HALO + ts, cw), F32)],
        compiler_params=_params("parallel", "parallel"),
        name="conformer_conv",
    )(proj, proj, proj, proj, w, b.reshape(1, cw), ln_g.reshape(1, cw), ln_b.reshape(1, cw))


def _outproj_kernel(at_ref, cv_ref, x_ref, w_ref, g_ref, b_ref, o_ref, ot_ref, *, alpha):
    aw = at_ref.shape[1]
    mixed = jnp.dot(at_ref[...], w_ref[0:aw, :], preferred_element_type=F32)
    mixed = mixed + jnp.dot(cv_ref[...], w_ref[aw:, :], preferred_element_type=F32)
    y = _layer_norm_rows(alpha * x_ref[...] + mixed, g_ref[...], b_ref[...])
    o_ref[...] = y
    ot_ref[...] = y.T.astype(BF16)


def _outproj(attn, conv, x, w_out, g, b, alpha, tm):
    t, d = x.shape
    aw = attn.shape[1]
    cw = conv.shape[1]
    vec = pl.BlockSpec((1, d), lambda i: (0, 0))
    return pl.pallas_call(
        functools.partial(_outproj_kernel, alpha=alpha),
        grid=(t // tm,),
        in_specs=[pl.BlockSpec((tm, aw), lambda i: (i, 0)),
                  pl.BlockSpec((tm, cw), lambda i: (i, 0)),
                  pl.BlockSpec((tm, d), lambda i: (i, 0)),
                  pl.BlockSpec((aw + cw, d), lambda i: (0, 0)), vec, vec],
        out_specs=[pl.BlockSpec((tm, d), lambda i: (i, 0)),
                   pl.BlockSpec((d, tm), lambda i: (0, i))],
        out_shape=[jax.ShapeDtypeStruct((t, d), F32),
                   jax.ShapeDtypeStruct((d, t), BF16)],
        compiler_params=_params("parallel"),
        name="outproj_ln",
    )(attn, conv, x, w_out, g.reshape(1, d), b.reshape(1, d))


def _fold_kernel(k_ref, w_ref, o_ref):
    o_ref[...] = lax.dot_general(k_ref[0], w_ref[...], (((1,), (1,)), ((), ())),
                                 precision=lax.Precision.HIGHEST,
                                 preferred_element_type=F32).astype(o_ref.dtype)


def _fold_keys(keys, wq):
    d = wq.shape[0]
    groups = keys.shape[0]
    return pl.pallas_call(
        _fold_kernel,
        grid=(groups,),
        in_specs=[pl.BlockSpec((1, PEER_NKEYS, PEER_HALF), lambda i: (i, 0, 0)),
                  pl.BlockSpec((d, PEER_HALF), lambda i: (0, i))],
        out_specs=pl.BlockSpec((PEER_NKEYS, d), lambda i: (i, 0)),
        out_shape=jax.ShapeDtypeStruct((groups * PEER_NKEYS, d), BF16),
        compiler_params=_params("parallel"),
        name="peer_fold_keys",
    )(keys, wq)


assert PEER_TOPK == 2 * SUBLANES


def _batcher_pairs(n):
    pairs = []
    p = 1
    while p < n:
        k = p
        while k >= 1:
            for j in range(k % p, n - k, 2 * k):
                for i in range(min(k, n - j - k)):
                    if (i + j) // (2 * p) == (i + j + k) // (2 * p):
                        pairs.append((i + j, i + j + k))
            k //= 2
        p *= 2
    return pairs


def _top_values(tiles, k):
    n = 1
    while n < len(tiles):
        n *= 2
    tiles = list(tiles) + [None] * (n - len(tiles))
    for lo, hi in _batcher_pairs(n):
        a, b = tiles[lo], tiles[hi]
        if b is None:
            continue
        if a is None:
            tiles[lo], tiles[hi] = b, None
        else:
            tiles[lo], tiles[hi] = jnp.maximum(a, b), jnp.minimum(a, b)
    tiles = [t for t in tiles if t is not None]
    tops = []
    for _ in range(k):
        mx = jnp.max(tiles[0], axis=0, keepdims=True)
        hit = tiles[0] == mx
        for i in range(len(tiles) - 1):
            tiles[i] = jnp.where(hit, tiles[i + 1], tiles[i])
        tiles[-1] = jnp.where(hit, -jnp.inf, tiles[-1])
        tops.append(mx)
    return tops


def _row_tiles(x):
    return [x[i:i + SUBLANES, :] for i in range(0, x.shape[0], SUBLANES)]


def _route_kernel(s_ref, r2_ref, e2_ref, cnt_ref, e1_ref):
    nk = PEER_NKEYS
    k = PEER_TOPK
    sub = SUBLANES
    n = s_ref.shape[1]
    row = lax.broadcasted_iota(jnp.int32, (sub, n), 0)

    def head(h, carry):
        base = pl.multiple_of(h * 2 * nk, 2 * nk)
        s1 = s_ref[pl.ds(base, nk), :]
        s2 = s_ref[pl.ds(base + nk, nk), :]
        a = jnp.concatenate(_top_values(_row_tiles(s1), k), axis=0)
        b = jnp.concatenate(_top_values(_row_tiles(s2), k), axis=0)
        rank2 = jnp.full(s2.shape, float(k), F32)
        for r in range(k):
            rank2 = jnp.where(s2 == b[r:r + 1, :], float(r), rank2)
        cands = [a + b[0:1, :]]
        for jj in range(1, sub):
            keep = k // (jj + 1)
            v = a[0:sub, :] + b[jj:jj + 1, :]
            cands.append(v if keep >= sub else jnp.where(row < keep, v, -jnp.inf))
        cands.append(a[0:1, :] + b[sub:k, :])
        z = jnp.concatenate(cands, axis=0)
        best = _top_values(_row_tiles(z), k)
        chosen = z >= best[k - 1]
        zmax = best[0]
        denom = jnp.sum(jnp.where(chosen, jnp.exp(z - zmax), 0.0), axis=0, keepdims=True)
        ones = jnp.where(chosen, 1.0, 0.0)
        low = ones[0:sub, :]
        for jj in range(1, sub):
            low = low + ones[k + (jj - 1) * sub:k + jj * sub, :]
        tail = jnp.sum(ones[k + (sub - 1) * sub:, :], axis=0, keepdims=True)
        low = low + jnp.where(row == 0, tail, 0.0)
        cnt_sorted = jnp.concatenate([low, ones[sub:k, :]], axis=0)
        cnt = jnp.zeros(s1.shape, F32)
        for i in range(k):
            cnt = jnp.where(s1 == a[i:i + 1, :], cnt_sorted[i:i + 1, :], cnt)
        out = pl.multiple_of(h * nk, nk)
        r2_ref[pl.ds(out, nk), :] = rank2.astype(BF16)
        e2_ref[pl.ds(out, nk), :] = jnp.exp(s2 - b[0:1, :]).astype(BF16)
        cnt_ref[pl.ds(out, nk), :] = cnt
        e1_ref[pl.ds(out, nk), :] = jnp.exp(s1 - a[0:1, :]) * (1.0 / denom)
        return carry

    lax.fori_loop(0, PEER_HEADS, head, 0)


def _route(scores_t, tl):
    rows, t = scores_t.shape
    out_rows = rows // 2
    spec = pl.BlockSpec((out_rows, tl), lambda i: (0, i))
    shape = lambda dtype: jax.ShapeDtypeStruct((out_rows, t), dtype)
    return pl.pallas_call(
        _route_kernel,
        grid=(t // tl,),
        in_specs=[pl.BlockSpec((rows, tl), lambda i: (0, i))],
        out_specs=[spec, spec, spec, spec],
        out_shape=[shape(BF16), shape(BF16), shape(F32), shape(F32)],
        compiler_params=_params("parallel"),
        name="peer_route",
    )(scores_t)


_GATE_ROWS = 16
_GATE_LANES = 256


def _gate_block(ht_scr, a_scr, r2_ref, e2_ref, cnt_ref, e1_ref, key, c, lh):
    nk = PEER_NKEYS
    shape = (_GATE_ROWS, _GATE_LANES)
    ls = pl.ds(lh * _GATE_LANES, _GATE_LANES)
    cnt_b = [jnp.broadcast_to(cnt_ref[h, key:key + 1, ls], shape).astype(BF16)
             for h in range(PEER_HEADS)]
    e1_b = [jnp.broadcast_to(e1_ref[h, key:key + 1, ls], shape).astype(BF16)
            for h in range(PEER_HEADS)]
    for r in range(nk // _GATE_ROWS):
        gate = None
        for h in range(PEER_HEADS):
            rs = pl.ds(h * nk + r * _GATE_ROWS, _GATE_ROWS)
            e2 = e2_ref[rs, ls]
            term = jnp.where(r2_ref[rs, ls] < cnt_b[h], e2, jnp.zeros_like(e2)) * e1_b[h]
            gate = term if gate is None else gate + term
        hs = pl.ds(c * nk + r * _GATE_ROWS, _GATE_ROWS)
        hv = ht_scr[hs, ls].astype(BF16)
        half = hv * 0.5
        a_scr[hs, ls] = gate * (half + half * lax.erf(hv * (2.0 ** -0.5)))


def _peer_kernel(u_ref, xt_ref, vt_ref, r2_ref, e2_ref, cnt_prev, cnt_cur, e1_prev, e1_cur,
                 x_ref, g_ref, b_ref, o_ref, acc_scr, ht_a, ht_b, a_a, a_b, *, alpha):
    tn, tt = ht_a.shape
    groups = tn // PEER_NKEYS
    p = pl.program_id(1)

    @pl.when(p == 0)
    def _():
        acc_scr[...] = jnp.zeros_like(acc_scr)
        ht_b[...] = jnp.zeros_like(ht_b)
        a_a[...] = jnp.zeros_like(a_a)

    def half_step(u_lo, a_in, ht_in, a_out, ht_out, cnt_ref, e1_ref, first_key):
        for lh in range(tt // _GATE_LANES):
            ls = pl.ds(lh * _GATE_LANES, _GATE_LANES)
            acc_scr[:, ls] += jnp.dot(vt_ref[:, u_lo:u_lo + tn], a_in[:, ls],
                                      preferred_element_type=F32)
            for c in range(groups):
                if c == groups // 2:
                    ht_out[:, ls] = jnp.dot(u_ref[u_lo:u_lo + tn, :], xt_ref[:, ls],
                                            preferred_element_type=F32)
                _gate_block(ht_in, a_out, r2_ref, e2_ref, cnt_ref, e1_ref, first_key + c, c, lh)

    half_step(0, a_a, ht_b, a_b, ht_a, cnt_prev, e1_prev, groups)
    half_step(tn, a_b, ht_a, a_a, ht_b, cnt_cur, e1_cur, 0)

    @pl.when(p == pl.num_programs(1) - 1)
    def _():
        y = alpha * x_ref[...] + acc_scr[...].T
        o_ref[...] = _layer_norm_rows(y, g_ref[...], b_ref[...])


def _peer(u_b, x_t, v_t, r2, e2, cnt, e1, x, g, b, alpha, tt, tn):
    n, d = u_b.shape
    t = x.shape[0]
    n_pairs = n // (2 * tn)
    groups = tn // PEER_NKEYS
    cnt3 = cnt.reshape(PEER_HEADS, PEER_NKEYS, t)
    e13 = e1.reshape(PEER_HEADS, PEER_NKEYS, t)
    this_pair = lambda p: jnp.minimum(p, n_pairs - 1)
    last_pair = lambda p: jnp.maximum(p - 1, 0)
    tok = pl.BlockSpec((PEER_HEADS * PEER_NKEYS, tt), lambda j, p: (0, j))
    keys = lambda which: pl.BlockSpec((PEER_HEADS, 2 * groups, tt), lambda j, p: (0, which(p), j))
    vec = pl.BlockSpec((1, d), lambda j, p: (0, 0))
    return pl.pallas_call(
        functools.partial(_peer_kernel, alpha=alpha),
        grid=(t // tt, n_pairs + 1),
        in_specs=[pl.BlockSpec((2 * tn, d), lambda j, p: (this_pair(p), 0)),
                  pl.BlockSpec((d, tt), lambda j, p: (0, j)),
                  pl.BlockSpec((d, 2 * tn), lambda j, p: (0, last_pair(p))),
                  tok, tok, keys(last_pair), keys(this_pair), keys(last_pair), keys(this_pair),
                  pl.BlockSpec((tt, d), lambda j, p: (j, 0)), vec, vec],
        out_specs=pl.BlockSpec((tt, d), lambda j, p: (j, 0)),
        out_shape=jax.ShapeDtypeStruct((t, d), F32),
        scratch_shapes=[pltpu.VMEM((d, tt), F32),
                        pltpu.VMEM((tn, tt), F32), pltpu.VMEM((tn, tt), F32),
                        pltpu.VMEM((tn, tt), BF16), pltpu.VMEM((tn, tt), BF16)],
        compiler_params=_params("parallel", "arbitrary"),
        name="peer_experts",
    )(u_b, x_t, v_t, r2, e2, cnt3, cnt3, e13, e13, x, g.reshape(1, d), b.reshape(1, d))


def _rope_tables(seq):
    pos = jnp.arange(seq, dtype=F32)
    inv = jnp.power(ROPE_THETA, -jnp.arange(0, HEAD_DIM, 2, dtype=F32) / HEAD_DIM)
    ang = pos[:, None] * inv[None, :]
    cos = jnp.tile(jnp.cos(ang), (1, 2 * ATTN_HEADS))
    sin = jnp.tile(jnp.sin(ang), (1, 2 * ATTN_HEADS))
    return cos, sin


def kernel(x, w_in, conv_w, conv_b, conv_ln_g, conv_ln_b, w_out, ln1_g, ln1_b,
           peer_wq, peer_keys, peer_u, peer_v, ln2_g, ln2_b):
    batch, seq, d = x.shape
    depth = w_in.shape[0]
    t = batch * seq
    aw = ATTN_WIDTH
    alpha = (2 * depth) ** 0.25
    nb = seq // MOBA_BLOCK

    cos, sin = _rope_tables(seq)
    eye = (jnp.arange(seq + MOBA_BLOCK)[:, None] // MOBA_BLOCK
           == jnp.arange(LANES)[None, :]).astype(BF16)

    xf = x.reshape(t, d)
    for l in range(depth):
        qt, kb, vt, kmean, ag = _inproj(xf, w_in[l].astype(BF16), cos, sin, seq)
        attn = _moba(qt, kb, vt, kmean.reshape(batch, nb, aw), eye, batch, seq)
        conv = _conv(ag, 0, 1, conv_w[l], conv_b[l], conv_ln_g[l], conv_ln_b[l],
                     batch, seq, ts=512)
        x1, x1t = _outproj(attn, conv, xf, w_out[l].astype(BF16), ln1_g[l], ln1_b[l], alpha, tm=256)

        groups = PEER_HEADS * 2
        ws_t = _fold_keys(peer_keys[l].reshape(groups, PEER_NKEYS, PEER_HALF), peer_wq[l])
        scores_t = _matmul(ws_t, x1t, tm=ws_t.shape[0], tn=512, out_dtype=F32, name="peer_scores")
        r2, e2, cnt, e1 = _route(scores_t, tl=256)
        xf = _peer(peer_u[l].astype(BF16), x1t, peer_v[l].astype(BF16).T,
                   r2, e2, cnt, e1, x1, ln2_g[l], ln2_b[l], alpha, tt=512, tn=512)
    return xf.reshape(batch, seq, d)
```

```python
import functools

import jax
import jax.numpy as jnp
from jax import lax
from jax.experimental import pallas as pl
from jax.experimental.pallas import tpu as pltpu

ATTN_HEADS = 8
HEAD_DIM = 64
ATTN_WIDTH = ATTN_HEADS * HEAD_DIM
CONV_KERNEL = 31
ROPE_THETA = 10000.0
MOBA_BLOCK = 256
MOBA_TOPK = 3
PEER_HEADS = 8
PEER_NKEYS = 128
PEER_HALF = 128
PEER_TOPK = 16
LN_EPS = 1e-5

LANES = 128
SUBLANES = 8
VMEM_LIMIT_BYTES = 56 * 1024 * 1024

NEG_BIG = -1e30
CONV_HALO = 32

F32 = jnp.float32
BF16 = jnp.bfloat16


def _params(*semantics):
    return pltpu.CompilerParams(dimension_semantics=semantics,
                                vmem_limit_bytes=VMEM_LIMIT_BYTES)


def _layer_norm_rows(y, g, b):
    mu = jnp.mean(y, axis=-1, keepdims=True)
    yc = y - mu
    var = jnp.mean(yc * yc, axis=-1, keepdims=True)
    return yc * lax.rsqrt(var + LN_EPS) * g + b


def _matmul_kernel(a_ref, b_ref, o_ref):
    a = a_ref[...].astype(BF16)
    o_ref[...] = jnp.dot(a, b_ref[...], preferred_element_type=F32).astype(o_ref.dtype)


def _matmul(a, b, *, tm, tn, out_dtype, name):
    m, k = a.shape
    _, n = b.shape
    return pl.pallas_call(
        _matmul_kernel,
        grid=(m // tm, n // tn),
        in_specs=[pl.BlockSpec((tm, k), lambda i, j: (i, 0)),
                  pl.BlockSpec((k, tn), lambda i, j: (0, j))],
        out_specs=pl.BlockSpec((tm, tn), lambda i, j: (i, j)),
        out_shape=jax.ShapeDtypeStruct((m, n), out_dtype),
        compiler_params=_params("parallel", "parallel"),
        name=name,
    )(a, b)


def _rotate_half(x):
    n = x.shape[1]
    half = HEAD_DIM // 2
    col = lax.broadcasted_iota(jnp.int32, x.shape, 1)
    from_above = pltpu.roll(x, n - half, 1)
    from_below = pltpu.roll(x, half, 1)
    return jnp.where(col % HEAD_DIM < half, -from_above, from_below)


def _inproj_kernel(x_ref, w_ref, cos_ref, sin_ref, qt_ref, ko_ref, vt_ref, km_ref, ag_ref):
    w = ATTN_WIDTH
    proj = jnp.dot(x_ref[...].astype(BF16), w_ref[...], preferred_element_type=F32)
    cos = cos_ref[...]
    sin = sin_ref[...]
    q = proj[:, 0:w]
    k = proj[:, w:2 * w]
    qt_ref[...] = (q * cos + _rotate_half(q) * sin).T
    k = k * cos + _rotate_half(k) * sin
    ko_ref[...] = k.astype(BF16)
    vt_ref[...] = proj[:, 2 * w:3 * w].T.astype(BF16)
    km_ref[0] = jnp.mean(k, axis=0, keepdims=True)
    ag_ref[...] = proj[:, 3 * w:]


def _inproj(x, w_ext, cos, sin, seq):
    t, d = x.shape
    w = ATTN_WIDTH
    n = w_ext.shape[1]
    nblk = t // MOBA_BLOCK
    per_seq = seq // MOBA_BLOCK
    tab = pl.BlockSpec((MOBA_BLOCK, w), lambda i: (i % per_seq, 0))
    colmajor = pl.BlockSpec((w, MOBA_BLOCK), lambda i: (0, i))
    return pl.pallas_call(
        _inproj_kernel,
        grid=(nblk,),
        in_specs=[pl.BlockSpec((MOBA_BLOCK, d), lambda i: (i, 0)),
                  pl.BlockSpec((d, n), lambda i: (0, 0)), tab, tab],
        out_specs=[colmajor, pl.BlockSpec((MOBA_BLOCK, w), lambda i: (i, 0)), colmajor,
                   pl.BlockSpec((1, 1, w), lambda i: (i, 0, 0)),
                   pl.BlockSpec((MOBA_BLOCK, n - 3 * w), lambda i: (i, 0))],
        out_shape=[jax.ShapeDtypeStruct((w, t), F32),
                   jax.ShapeDtypeStruct((t, w), BF16),
                   jax.ShapeDtypeStruct((w, t), BF16),
                   jax.ShapeDtypeStruct((nblk, 1, w), F32),
                   jax.ShapeDtypeStruct((t, n - 3 * w), F32)],
        compiler_params=_params("parallel"),
        name="in_proj_rope",
    )(x, w_ext, cos, sin)


def _tree_reduce(op, s):
    r = s.shape[0]
    while r > SUBLANES:
        r //= 2
        s = op(s[0:r, :], s[r:2 * r, :])
    if op is jnp.add:
        return jnp.sum(s, axis=0, keepdims=True)
    return jnp.max(s, axis=0, keepdims=True)


def _moba_kernel(qt_ref, k_ref, vt_ref, km_ref, e_ref, o_ref, acc0, acc1, sa0, sa1, sb0, sb1,
                 *, scale, n_sel):
    blk = MOBA_BLOCK
    nb = km_ref.shape[1]
    seq = k_ref.shape[0]
    i = pl.program_id(2)
    qt = qt_ref[...]
    km = km_ref[0]
    chan = lax.broadcasted_iota(jnp.int32, (LANES, blk), 0)
    km_chan = lax.broadcasted_iota(jnp.int32, (nb, LANES), 1)
    blk_ix = lax.broadcasted_iota(jnp.int32, (nb, blk), 0)
    pad_ix = lax.broadcasted_iota(jnp.int32, (LANES - nb, blk), 0)
    accs = (acc0, acc1)
    slots = ((sa0, sa1), (sb0, sb1))

    qexts = []
    for h in range(2):
        kmh = jnp.where(km_chan // HEAD_DIM == h, km, 0.0)
        gate = jnp.dot(kmh, qt, precision=lax.Precision.HIGHEST,
                       preferred_element_type=F32)
        g = jnp.where(blk_ix < i, gate, -jnp.inf)
        bias = jnp.full((nb, blk), NEG_BIG, F32)
        for _ in range(n_sel):
            mx = jnp.max(g, axis=0, keepdims=True)
            first = jnp.min(jnp.where(g == mx, blk_ix, nb), axis=0, keepdims=True)
            first = jnp.where(mx > -jnp.inf, first, nb)
            pick = blk_ix == first
            bias = jnp.where(pick, 0.0, bias)
            g = jnp.where(pick, -jnp.inf, g)
        qh = (jnp.where(chan // HEAD_DIM == h, qt, 0.0) * scale).astype(BF16)
        pad = jnp.where(pad_ix == 0, NEG_BIG, 0.0).astype(BF16)
        qexts.append(jnp.concatenate([qh, bias.astype(BF16), pad], axis=0))

    own = pl.multiple_of(i * blk, blk)
    kd = k_ref[pl.ds(own, blk), :]
    vd = vt_ref[:, pl.ds(own, blk)]
    key_ix = lax.broadcasted_iota(jnp.int32, (blk, blk), 0)
    qry_ix = lax.broadcasted_iota(jnp.int32, (blk, blk), 1)
    stats = []
    ss = [jnp.dot(kd, qexts[h][0:LANES, :], preferred_element_type=F32) for h in range(2)]
    for h in range(2):
        s = jnp.where(key_ix <= qry_ix, ss[h], NEG_BIG)
        m = _tree_reduce(jnp.maximum, s)
        p = jnp.exp(s - m)
        stats += [m, _tree_reduce(jnp.add, p)]
        accs[h][...] = jnp.dot(vd, p.astype(BF16), preferred_element_type=F32)

    def key_offset(j):
        return pl.multiple_of(jnp.where(j < i, j, 0) * blk, blk)

    def scores(j):
        e_off = pl.multiple_of(jnp.where(j < i, j * blk, seq), blk)
        kext = jnp.concatenate([k_ref[pl.ds(key_offset(j), blk), :], e_ref[pl.ds(e_off, blk), :]],
                               axis=1)
        return [jnp.dot(kext, qexts[h], preferred_element_type=F32) for h in range(2)]

    def consume(slot, j, carry):
        vj = vt_ref[:, pl.ds(key_offset(j), blk)]
        out = []
        for h in range(2):
            m_prev, l_prev = carry[2 * h], carry[2 * h + 1]
            s = slot[h][...]
            m_new = jnp.maximum(m_prev, _tree_reduce(jnp.maximum, s))
            alpha = jnp.exp(m_prev - m_new)
            p = jnp.exp(s - m_new)
            accs[h][...] = alpha * accs[h][...] + jnp.dot(vj, p.astype(BF16),
                                                          preferred_element_type=F32)
            out += [m_new, alpha * l_prev + _tree_reduce(jnp.add, p)]
        return tuple(out)

    def fill(slot, ss):
        for h in range(2):
            slot[h][...] = ss[h]

    fill(slots[0], scores(0))

    def body(jj, carry):
        j0 = 2 * jj
        nxt = scores(j0 + 1)
        carry = consume(slots[0], j0, carry)
        fill(slots[1], nxt)
        nxt = scores(j0 + 2)
        carry = consume(slots[1], j0 + 1, carry)
        fill(slots[0], nxt)
        return carry

    stats = lax.fori_loop(0, (i + 1) // 2, body, tuple(stats))
    o0 = acc0[0:HEAD_DIM, :] / stats[1]
    o1 = acc1[HEAD_DIM:, :] / stats[3]
    o_ref[...] = jnp.concatenate([o0, o1], axis=0).T.astype(o_ref.dtype)


def _moba(qt, kb, vt, kmean, eye, batch, seq):
    t = kb.shape[0]
    blk = MOBA_BLOCK
    nb = seq // blk
    n_sel = max(1, min(MOBA_TOPK, nb - 1))
    pairs = ATTN_WIDTH // LANES
    kernel = functools.partial(_moba_kernel, scale=HEAD_DIM ** -0.5, n_sel=n_sel)
    score_buf = pltpu.VMEM((blk, blk), F32)
    return pl.pallas_call(
        kernel,
        grid=(batch, pairs, nb),
        in_specs=[pl.BlockSpec((LANES, blk), lambda b, p, i: (p, b * nb + i)),
                  pl.BlockSpec((seq, LANES), lambda b, p, i: (b, p)),
                  pl.BlockSpec((LANES, seq), lambda b, p, i: (p, b)),
                  pl.BlockSpec((1, nb, LANES), lambda b, p, i: (b, 0, p)),
                  pl.BlockSpec((seq + blk, LANES), lambda b, p, i: (0, 0))],
        out_specs=pl.BlockSpec((blk, LANES), lambda b, p, i: (b * nb + i, p)),
        out_shape=jax.ShapeDtypeStruct((t, ATTN_WIDTH), BF16),
        scratch_shapes=[pltpu.VMEM((LANES, blk), F32), pltpu.VMEM((LANES, blk), F32),
                        score_buf, score_buf, score_buf, score_buf],
        compiler_params=_params("parallel", "parallel", "arbitrary"),
        name="moba_attention",
    )(qt, kb, vt, kmean, eye)


def _conv_kernel(a_ref, g_ref, ah_ref, gh_ref, w_ref, b_ref, lg_ref, lb_ref, o_ref, u_scr,
                 *, ts):
    j = pl.program_id(1)
    halo = ah_ref[...] * jax.nn.sigmoid(gh_ref[...])
    u_scr[0, 0:CONV_HALO, :] = jnp.where(j > 0, halo, 0.0)
    u_scr[0, CONV_HALO:, :] = a_ref[...] * jax.nn.sigmoid(g_ref[...])
    rows = CONV_HALO + ts
    for r in range(1, SUBLANES):
        u_scr[r, 0:rows - SUBLANES, :] = u_scr[0, r:r + rows - SUBLANES, :]
    first = CONV_HALO - (CONV_KERNEL - 1)
    acc = jnp.zeros((ts, a_ref.shape[1]), F32)
    for tap in range(CONV_KERNEL):
        r = (first + tap) % SUBLANES
        base = first + tap - r
        acc = acc + w_ref[tap:tap + 1, :] * u_scr[r, base:base + ts, :]
    y = _layer_norm_rows(acc + b_ref[...], lg_ref[...], lb_ref[...])
    o_ref[...] = (y * jax.nn.sigmoid(y)).astype(o_ref.dtype)


def _conv(proj, a_col, g_col, w, b, ln_g, ln_b, batch, seq, ts):
    t = proj.shape[0]
    cw = w.shape[1]
    per_seq = seq // ts
    halo_per_tile = ts // CONV_HALO
    cur = lambda c: pl.BlockSpec((ts, cw), lambda bi, j, c=c: (bi * per_seq + j, c))
    halo = lambda c: pl.BlockSpec(
        (CONV_HALO, cw),
        lambda bi, j, c=c: (jnp.maximum((bi * per_seq + j) * halo_per_tile - 1, 0), c))
    vec = pl.BlockSpec((1, cw), lambda bi, j: (0, 0))
    return pl.pallas_call(
        functools.partial(_conv_kernel, ts=ts),
        grid=(batch, per_seq),
        in_specs=[cur(a_col), cur(g_col), halo(a_col), halo(g_col),
                  pl.BlockSpec((CONV_KERNEL, cw), lambda bi, j: (0, 0)), vec, vec, vec],
        out_specs=pl.BlockSpec((ts, cw), lambda bi, j: (bi * per_seq + j, 0)),
        out_shape=jax.ShapeDtypeStruct((t, cw), BF16),
        scratch_shapes=[pltpu.VMEM((SUBLANES, CONV_HALO + ts, cw), F32)],
        compiler_params=_params("parallel", "parallel"),
        name="conformer_conv",
    )(proj, proj, proj, proj, w, b.reshape(1, cw), ln_g.reshape(1, cw), ln_b.reshape(1, cw))


def _outproj_kernel(at_ref, cv_ref, x_ref, w_ref, g_ref, b_ref, o_ref, ot_ref, *, alpha):
    aw = at_ref.shape[1]
    mixed = jnp.dot(at_ref[...], w_ref[0:aw, :], preferred_element_type=F32)
    mixed = mixed + jnp.dot(cv_ref[...], w_ref[aw:, :], preferred_element_type=F32)
    y = _layer_norm_rows(alpha * x_ref[...] + mixed, g_ref[...], b_ref[...])
    o_ref[...] = y
    ot_ref[...] = y.T.astype(BF16)


def _outproj(attn, conv, x, w_out, g, b, alpha, tm):
    t, d = x.shape
    aw = attn.shape[1]
    cw = conv.shape[1]
    vec = pl.BlockSpec((1, d), lambda i: (0, 0))
    return pl.pallas_call(
        functools.partial(_outproj_kernel, alpha=alpha),
        grid=(t // tm,),
        in_specs=[pl.BlockSpec((tm, aw), lambda i: (i, 0)),
                  pl.BlockSpec((tm, cw), lambda i: (i, 0)),
                  pl.BlockSpec((tm, d), lambda i: (i, 0)),
                  pl.BlockSpec((aw + cw, d), lambda i: (0, 0)), vec, vec],
        out_specs=[pl.BlockSpec((tm, d), lambda i: (i, 0)),
                   pl.BlockSpec((d, tm), lambda i: (0, i))],
        out_shape=[jax.ShapeDtypeStruct((t, d), F32),
                   jax.ShapeDtypeStruct((d, t), BF16)],
        compiler_params=_params("parallel"),
        name="outproj_ln",
    )(attn, conv, x, w_out, g.reshape(1, d), b.reshape(1, d))


def _fold_kernel(k_ref, w_ref, o_ref):
    o_ref[...] = lax.dot_general(k_ref[0], w_ref[...], (((1,), (1,)), ((), ())),
                                 precision=lax.Precision.HIGHEST,
                                 preferred_element_type=F32).astype(o_ref.dtype)


def _fold_keys(keys, wq):
    d = wq.shape[0]
    groups = keys.shape[0]
    return pl.pallas_call(
        _fold_kernel,
        grid=(groups,),
        in_specs=[pl.BlockSpec((1, PEER_NKEYS, PEER_HALF), lambda i: (i, 0, 0)),
                  pl.BlockSpec((d, PEER_HALF), lambda i: (0, i))],
        out_specs=pl.BlockSpec((PEER_NKEYS, d), lambda i: (i, 0)),
        out_shape=jax.ShapeDtypeStruct((groups * PEER_NKEYS, d), BF16),
        compiler_params=_params("parallel"),
        name="peer_fold_keys",
    )(keys, wq)


assert PEER_TOPK == 2 * SUBLANES


def _batcher_pairs(n):
    pairs = []
    p = 1
    while p < n:
        k = p
        while k >= 1:
            for j in range(k % p, n - k, 2 * k):
                for i in range(min(k, n - j - k)):
                    if (i + j) // (2 * p) == (i + j + k) // (2 * p):
                        pairs.append((i + j, i + j + k))
            k //= 2
        p *= 2
    return pairs


def _top_values(tiles, k):
    n = 1
    while n < len(tiles):
        n *= 2
    tiles = list(tiles) + [None] * (n - len(tiles))
    for lo, hi in _batcher_pairs(n):
        a, b = tiles[lo], tiles[hi]
        if b is None:
            continue
        if a is None:
            tiles[lo], tiles[hi] = b, None
        else:
            tiles[lo], tiles[hi] = jnp.maximum(a, b), jnp.minimum(a, b)
    tiles = [t for t in tiles if t is not None]
    tops = []
    for _ in range(k):
        mx = jnp.max(tiles[0], axis=0, keepdims=True)
        hit = tiles[0] == mx
        for i in range(len(tiles) - 1):
            tiles[i] = jnp.where(hit, tiles[i + 1], tiles[i])
        tiles[-1] = jnp.where(hit, -jnp.inf, tiles[-1])
        tops.append(mx)
    return tops


def _row_tiles(x):
    return [x[i:i + SUBLANES, :] for i in range(0, x.shape[0], SUBLANES)]


def _route_kernel(s_ref, r2_ref, e2_ref, cnt_ref, e1_ref):
    nk = PEER_NKEYS
    k = PEER_TOPK
    sub = SUBLANES
    n = s_ref.shape[1]
    row = lax.broadcasted_iota(jnp.int32, (sub, n), 0)

    def head(h, carry):
        base = pl.multiple_of(h * 2 * nk, 2 * nk)
        s1 = s_ref[pl.ds(base, nk), :]
        s2 = s_ref[pl.ds(base + nk, nk), :]
        a = jnp.concatenate(_top_values(_row_tiles(s1), k), axis=0)
        b = jnp.concatenate(_top_values(_row_tiles(s2), k), axis=0)
        rank2 = jnp.full(s2.shape, float(k), F32)
        for r in range(k):
            rank2 = jnp.where(s2 == b[r:r + 1, :], float(r), rank2)
        cands = [a + b[0:1, :]]
        for jj in range(1, sub):
            keep = k // (jj + 1)
            v = a[0:sub, :] + b[jj:jj + 1, :]
            cands.append(v if keep >= sub else jnp.where(row < keep, v, -jnp.inf))
        cands.append(a[0:1, :] + b[sub:k, :])
        z = jnp.concatenate(cands, axis=0)
        best = _top_values(_row_tiles(z), k)
        chosen = z >= best[k - 1]
        zmax = best[0]
        denom = jnp.sum(jnp.where(chosen, jnp.exp(z - zmax), 0.0), axis=0, keepdims=True)
        ones = jnp.where(chosen, 1.0, 0.0)
        low = ones[0:sub, :]
        for jj in range(1, sub):
            low = low + ones[k + (jj - 1) * sub:k + jj * sub, :]
        tail = jnp.sum(ones[k + (sub - 1) * sub:, :], axis=0, keepdims=True)
        low = low + jnp.where(row == 0, tail, 0.0)
        cnt_sorted = jnp.concatenate([low, ones[sub:k, :]], axis=0)
        cnt = jnp.zeros(s1.shape, F32)
        for i in range(k):
            cnt = jnp.where(s1 == a[i:i + 1, :], cnt_sorted[i:i + 1, :], cnt)
        out = pl.multiple_of(h * nk, nk)
        r2_ref[pl.ds(out, nk), :] = rank2.astype(BF16)
        e2_ref[pl.ds(out, nk), :] = jnp.exp(s2 - b[0:1, :]).astype(BF16)
        cnt_ref[pl.ds(out, nk), :] = cnt
        e1_ref[pl.ds(out, nk), :] = jnp.exp(s1 - a[0:1, :]) * (1.0 / denom)
        return carry

    lax.fori_loop(0, PEER_HEADS, head, 0)


def _route(scores_t, tl):
    rows, t = scores_t.shape
    out_rows = rows // 2
    spec = pl.BlockSpec((out_rows, tl), lambda i: (0, i))
    shape = lambda dtype: jax.ShapeDtypeStruct((out_rows, t), dtype)
    return pl.pallas_call(
        _route_kernel,
        grid=(t // tl,),
        in_specs=[pl.BlockSpec((rows, tl), lambda i: (0, i))],
        out_specs=[spec, spec, spec, spec],
        out_shape=[shape(BF16), shape(BF16), shape(F32), shape(F32)],
        compiler_params=_params("parallel"),
        name="peer_route",
    )(scores_t)


_GATE_ROWS = 16
_GATE_LANES = 256


def _gate_block(ht_scr, a_scr, r2_ref, e2_ref, cnt_ref, e1_ref, key, c, lh):
    nk = PEER_NKEYS
    shape = (_GATE_ROWS, _GATE_LANES)
    ls = pl.ds(lh * _GATE_LANES, _GATE_LANES)
    cnt_b = [jnp.broadcast_to(cnt_ref[h, key:key + 1, ls], shape).astype(BF16)
             for h in range(PEER_HEADS)]
    e1_b = [jnp.broadcast_to(e1_ref[h, key:key + 1, ls], shape).astype(BF16)
            for h in range(PEER_HEADS)]
    for r in range(nk // _GATE_ROWS):
        gate = None
        for h in range(PEER_HEADS):
            rs = pl.ds(h * nk + r * _GATE_ROWS, _GATE_ROWS)
            e2 = e2_ref[rs, ls]
            term = jnp.where(r2_ref[rs, ls] < cnt_b[h], e2, jnp.zeros_like(e2)) * e1_b[h]
            gate = term if gate is None else gate + term
        hs = pl.ds(c * nk + r * _GATE_ROWS, _GATE_ROWS)
        hv = ht_scr[hs, ls].astype(BF16)
        half = hv * 0.5
        a_scr[hs, ls] = gate * (half + half * lax.erf(hv * (2.0 ** -0.5)))


def _peer_kernel(u_ref, xt_ref, vt_ref, r2_ref, e2_ref, cnt_prev, cnt_cur, e1_prev, e1_cur,
                 x_ref, g_ref, b_ref, o_ref, acc_scr, ht_a, ht_b, a_a, a_b, *, alpha):
    tn, tt = ht_a.shape
    groups = tn // PEER_NKEYS
    p = pl.program_id(1)

    @pl.when(p == 0)
    def _():
        acc_scr[...] = jnp.zeros_like(acc_scr)
        ht_b[...] = jnp.zeros_like(ht_b)
        a_a[...] = jnp.zeros_like(a_a)

    def half_step(u_lo, a_in, ht_in, a_out, ht_out, cnt_ref, e1_ref, first_key):
        for lh in range(tt // _GATE_LANES):
            ls = pl.ds(lh * _GATE_LANES, _GATE_LANES)
            acc_scr[:, ls] += jnp.dot(vt_ref[:, u_lo:u_lo + tn], a_in[:, ls],
                                      preferred_element_type=F32)
            for c in range(groups):
                if c == groups // 2:
                    ht_out[:, ls] = jnp.dot(u_ref[u_lo:u_lo + tn, :], xt_ref[:, ls],
                                            preferred_element_type=F32)
                _gate_block(ht_in, a_out, r2_ref, e2_ref, cnt_ref, e1_ref, first_key + c, c, lh)

    half_step(0, a_a, ht_b, a_b, ht_a, cnt_prev, e1_prev, groups)
    half_step(tn, a_b, ht_a, a_a, ht_b, cnt_cur, e1_cur, 0)

    @pl.when(p == pl.num_programs(1) - 1)
    def _():
        y = alpha * x_ref[...] + acc_scr[...].T
        o_ref[...] = _layer_norm_rows(y, g_ref[...], b_ref[...])


def _peer(u_all, x_t, vt_all, layer, r2, e2, cnt, e1, x, g, b, alpha, tt, tn):
    _, n, d = u_all.shape
    t = x.shape[0]
    n_pairs = n // (2 * tn)
    groups = tn // PEER_NKEYS
    cnt3 = cnt.reshape(PEER_HEADS, PEER_NKEYS, t)
    e13 = e1.reshape(PEER_HEADS, PEER_NKEYS, t)
    this_pair = lambda p: jnp.minimum(p, n_pairs - 1)
    last_pair = lambda p: jnp.maximum(p - 1, 0)
    tok = pl.BlockSpec((PEER_HEADS * PEER_NKEYS, tt), lambda j, p: (0, j))
    keys = lambda which: pl.BlockSpec((PEER_HEADS, 2 * groups, tt), lambda j, p: (0, which(p), j))
    vec = pl.BlockSpec((1, d), lambda j, p: (0, 0))
    return pl.pallas_call(
        functools.partial(_peer_kernel, alpha=alpha),
        grid=(t // tt, n_pairs + 1),
        in_specs=[pl.BlockSpec((None, 2 * tn, d), lambda j, p: (layer, this_pair(p), 0)),
                  pl.BlockSpec((d, tt), lambda j, p: (0, j)),
                  pl.BlockSpec((None, d, 2 * tn), lambda j, p: (layer, 0, last_pair(p))),
                  tok, tok, keys(last_pair), keys(this_pair), keys(last_pair), keys(this_pair),
                  pl.BlockSpec((tt, d), lambda j, p: (j, 0)), vec, vec],
        out_specs=pl.BlockSpec((tt, d), lambda j, p: (j, 0)),
        out_shape=jax.ShapeDtypeStruct((t, d), F32),
        scratch_shapes=[pltpu.VMEM((d, tt), F32),
                        pltpu.VMEM((tn, tt), F32), pltpu.VMEM((tn, tt), F32),
                        pltpu.VMEM((tn, tt), BF16), pltpu.VMEM((tn, tt), BF16)],
        compiler_params=_params("parallel", "arbitrary"),
        name="peer_experts",
    )(u_all, x_t, vt_all, r2, e2, cnt3, cnt3, e13, e13, x, g.reshape(1, d), b.reshape(1, d))


def _rope_tables(seq):
    pos = jnp.arange(seq, dtype=F32)
    inv = jnp.power(ROPE_THETA, -jnp.arange(0, HEAD_DIM, 2, dtype=F32) / HEAD_DIM)
    ang = pos[:, None] * inv[None, :]
    cos = jnp.tile(jnp.cos(ang), (1, 2 * ATTN_HEADS))
    sin = jnp.tile(jnp.sin(ang), (1, 2 * ATTN_HEADS))
    return cos, sin


def kernel(x, w_in, conv_w, conv_b, conv_ln_g, conv_ln_b, w_out, ln1_g, ln1_b,
           peer_wq, peer_keys, peer_u, peer_v, ln2_g, ln2_b):
    batch, seq, d = x.shape
    depth = w_in.shape[0]
    t = batch * seq
    aw = ATTN_WIDTH
    alpha = (2 * depth) ** 0.25
    nb = seq // MOBA_BLOCK

    cos, sin = _rope_tables(seq)
    eye = (jnp.arange(seq + MOBA_BLOCK)[:, None] // MOBA_BLOCK
           == jnp.arange(LANES)[None, :]).astype(BF16)

    u_all = peer_u.astype(BF16)
    vt_all = jnp.swapaxes(peer_v.astype(BF16), 1, 2)

    xf = x.reshape(t, d)
    for l in range(depth):
        qt, kb, vt, kmean, ag = _inproj(xf, w_in[l].astype(BF16), cos, sin, seq)
        attn = _moba(qt, kb, vt, kmean.reshape(batch, nb, aw), eye, batch, seq)
        conv = _conv(ag, 0, 1, conv_w[l], conv_b[l], conv_ln_g[l], conv_ln_b[l],
                     batch, seq, ts=512)
        x1, x1t = _outproj(attn, conv, xf, w_out[l].astype(BF16), ln1_g[l], ln1_b[l], alpha, tm=256)

        groups = PEER_HEADS * 2
        ws_t = _fold_keys(peer_keys[l].reshape(groups, PEER_NKEYS, PEER_HALF), peer_wq[l])
        scores_t = _matmul(ws_t, x1t, tm=ws_t.shape[0], tn=512, out_dtype=F32, name="peer_scores")
        r2, e2, cnt, e1 = _route(scores_t, tl=512)
        xf = _peer(u_all, x1t, vt_all, l, r2, e2, cnt, e1, x1, ln2_g[l], ln2_b[l], alpha,
                   tt=512, tn=512)
    return xf.reshape(batch, seq, d)
```

```python
import functools

import jax
import jax.numpy as jnp
from jax import lax
from jax.experimental import pallas as pl
from jax.experimental.pallas import tpu as pltpu

ATTN_HEADS = 8
HEAD_DIM = 64
ATTN_WIDTH = ATTN_HEADS * HEAD_DIM
CONV_KERNEL = 31
ROPE_THETA = 10000.0
MOBA_BLOCK = 256
MOBA_TOPK = 3
PEER_HEADS = 8
PEER_NKEYS = 128
PEER_HALF = 128
PEER_TOPK = 16
LN_EPS = 1e-5

LANES = 128
SUBLANES = 8
VMEM_LIMIT_BYTES = 56 * 1024 * 1024

NEG_BIG = -1e30
CONV_HALO = 32

F32 = jnp.float32
BF16 = jnp.bfloat16


def _params(*semantics):
    return pltpu.CompilerParams(dimension_semantics=semantics,
                                vmem_limit_bytes=VMEM_LIMIT_BYTES)


def _layer_norm_rows(y, g, b):
    mu = jnp.mean(y, axis=-1, keepdims=True)
    yc = y - mu
    var = jnp.mean(yc * yc, axis=-1, keepdims=True)
    return yc * lax.rsqrt(var + LN_EPS) * g + b


def _matmul_kernel(a_ref, b_ref, o_ref):
    a = a_ref[...].astype(BF16)
    o_ref[...] = jnp.dot(a, b_ref[...], preferred_element_type=F32).astype(o_ref.dtype)


def _matmul(a, b, *, tm, tn, out_dtype, name):
    m, k = a.shape
    _, n = b.shape
    return pl.pallas_call(
        _matmul_kernel,
        grid=(m // tm, n // tn),
        in_specs=[pl.BlockSpec((tm, k), lambda i, j: (i, 0)),
                  pl.BlockSpec((k, tn), lambda i, j: (0, j))],
        out_specs=pl.BlockSpec((tm, tn), lambda i, j: (i, j)),
        out_shape=jax.ShapeDtypeStruct((m, n), out_dtype),
        compiler_params=_params("parallel", "parallel"),
        name=name,
    )(a, b)


def _rotate_half(x):
    n = x.shape[1]
    half = HEAD_DIM // 2
    col = lax.broadcasted_iota(jnp.int32, x.shape, 1)
    from_above = pltpu.roll(x, n - half, 1)
    from_below = pltpu.roll(x, half, 1)
    return jnp.where(col % HEAD_DIM < half, -from_above, from_below)


def _inproj_kernel(x_ref, w_ref, cos_ref, sin_ref, qt_ref, ko_ref, vt_ref, km_ref, ag_ref):
    w = ATTN_WIDTH
    proj = jnp.dot(x_ref[...].astype(BF16), w_ref[...], preferred_element_type=F32)
    cos = cos_ref[...]
    sin = sin_ref[...]
    q = proj[:, 0:w]
    k = proj[:, w:2 * w]
    qt_ref[...] = (q * cos + _rotate_half(q) * sin).T
    k = k * cos + _rotate_half(k) * sin
    ko_ref[...] = k.astype(BF16)
    vt_ref[...] = proj[:, 2 * w:3 * w].T.astype(BF16)
    km_ref[0] = jnp.mean(k, axis=0, keepdims=True)
    ag_ref[...] = proj[:, 3 * w:]


def _inproj(x, w_ext, cos, sin, seq):
    t, d = x.shape
    w = ATTN_WIDTH
    n = w_ext.shape[1]
    nblk = t // MOBA_BLOCK
    per_seq = seq // MOBA_BLOCK
    tab = pl.BlockSpec((MOBA_BLOCK, w), lambda i: (i % per_seq, 0))
    colmajor = pl.BlockSpec((w, MOBA_BLOCK), lambda i: (0, i))
    return pl.pallas_call(
        _inproj_kernel,
        grid=(nblk,),
        in_specs=[pl.BlockSpec((MOBA_BLOCK, d), lambda i: (i, 0)),
                  pl.BlockSpec((d, n), lambda i: (0, 0)), tab, tab],
        out_specs=[colmajor, pl.BlockSpec((MOBA_BLOCK, w), lambda i: (i, 0)), colmajor,
                   pl.BlockSpec((1, 1, w), lambda i: (i, 0, 0)),
                   pl.BlockSpec((MOBA_BLOCK, n - 3 * w), lambda i: (i, 0))],
        out_shape=[jax.ShapeDtypeStruct((w, t), F32),
                   jax.ShapeDtypeStruct((t, w), BF16),
                   jax.ShapeDtypeStruct((w, t), BF16),
                   jax.ShapeDtypeStruct((nblk, 1, w), F32),
                   jax.ShapeDtypeStruct((t, n - 3 * w), F32)],
        compiler_params=_params("parallel"),
        name="in_proj_rope",
    )(x, w_ext, cos, sin)


def _tree_reduce(op, s):
    r = s.shape[0]
    while r > SUBLANES:
        r //= 2
        s = op(s[0:r, :], s[r:2 * r, :])
    if op is jnp.add:
        return jnp.sum(s, axis=0, keepdims=True)
    return jnp.max(s, axis=0, keepdims=True)


def _moba_kernel(qt_ref, k_ref, vt_ref, km_ref, e_ref, o_ref, *scratch, scale, n_sel, heads):
    blk = MOBA_BLOCK
    nb = km_ref.shape[1]
    seq = k_ref.shape[0]
    i = pl.program_id(2)
    accs = scratch[0:heads]
    slots = (scratch[heads:2 * heads], scratch[2 * heads:3 * heads])
    chan = lax.broadcasted_iota(jnp.int32, (LANES, blk), 0)
    km_chan = lax.broadcasted_iota(jnp.int32, (nb, LANES), 1)
    blk_ix = lax.broadcasted_iota(jnp.int32, (nb, blk), 0)
    pad_ix = lax.broadcasted_iota(jnp.int32, (LANES - nb, blk), 0)

    def pair_rows(h):
        return slice((h // 2) * LANES, (h // 2 + 1) * LANES)

    qexts = []
    for h in range(heads):
        qt = qt_ref[pair_rows(h), :]
        km = km_ref[0, :, pair_rows(h)]
        kmh = jnp.where(km_chan // HEAD_DIM == h % 2, km, 0.0)
        gate = jnp.dot(kmh, qt, precision=lax.Precision.HIGHEST,
                       preferred_element_type=F32)
        g = jnp.where(blk_ix < i, gate, -jnp.inf)
        bias = jnp.full((nb, blk), NEG_BIG, F32)
        for _ in range(n_sel):
            mx = jnp.max(g, axis=0, keepdims=True)
            first = jnp.min(jnp.where(g == mx, blk_ix, nb), axis=0, keepdims=True)
            first = jnp.where(mx > -jnp.inf, first, nb)
            pick = blk_ix == first
            bias = jnp.where(pick, 0.0, bias)
            g = jnp.where(pick, -jnp.inf, g)
        qh = (jnp.where(chan // HEAD_DIM == h % 2, qt, 0.0) * scale).astype(BF16)
        pad = jnp.where(pad_ix == 0, NEG_BIG, 0.0).astype(BF16)
        qexts.append(jnp.concatenate([qh, bias.astype(BF16), pad], axis=0))

    own = pl.multiple_of(i * blk, blk)
    key_ix = lax.broadcasted_iota(jnp.int32, (blk, blk), 0)
    qry_ix = lax.broadcasted_iota(jnp.int32, (blk, blk), 1)
    ss = [jnp.dot(k_ref[pl.ds(own, blk), pair_rows(h)], qexts[h][0:LANES, :],
                  preferred_element_type=F32) for h in range(heads)]
    stats = []
    for h in range(heads):
        s = jnp.where(key_ix <= qry_ix, ss[h], NEG_BIG)
        m = _tree_reduce(jnp.maximum, s)
        p = jnp.exp(s - m)
        stats += [m, _tree_reduce(jnp.add, p)]
        accs[h][...] = jnp.dot(vt_ref[pair_rows(h), pl.ds(own, blk)], p.astype(BF16),
                               preferred_element_type=F32)

    def key_offset(j):
        return pl.multiple_of(jnp.where(j < i, j, 0) * blk, blk)

    def scores(j):
        e_off = pl.multiple_of(jnp.where(j < i, j * blk, seq), blk)
        e_blk = e_ref[pl.ds(e_off, blk), :]
        kexts = [jnp.concatenate([k_ref[pl.ds(key_offset(j), blk), pair_rows(2 * pr)], e_blk], axis=1)
                 for pr in range(heads // 2)]
        return [jnp.dot(kexts[h // 2], qexts[h], preferred_element_type=F32) for h in range(heads)]

    def consume(slot, j, carry):
        out = []
        for h in range(heads):
            m_prev, l_prev = carry[2 * h], carry[2 * h + 1]
            s = slot[h][...]
            m_new = jnp.maximum(m_prev, _tree_reduce(jnp.maximum, s))
            alpha = jnp.exp(m_prev - m_new)
            p = jnp.exp(s - m_new)
            vj = vt_ref[pair_rows(h), pl.ds(key_offset(j), blk)]
            accs[h][...] = alpha * accs[h][...] + jnp.dot(vj, p.astype(BF16),
                                                          preferred_element_type=F32)
            out += [m_new, alpha * l_prev + _tree_reduce(jnp.add, p)]
        return tuple(out)

    def fill(slot, ss):
        for h in range(heads):
            slot[h][...] = ss[h]

    fill(slots[0], scores(0))

    def body(jj, carry):
        j0 = 2 * jj
        nxt = scores(j0 + 1)
        carry = consume(slots[0], j0, carry)
        fill(slots[1], nxt)
        nxt = scores(j0 + 2)
        carry = consume(slots[1], j0 + 1, carry)
        fill(slots[0], nxt)
        return carry

    stats = lax.fori_loop(0, (i + 1) // 2, body, tuple(stats))
    outs = []
    for h in range(heads):
        lo = (h % 2) * HEAD_DIM
        outs.append(accs[h][lo:lo + HEAD_DIM, :] / stats[2 * h + 1])
    o_ref[...] = jnp.concatenate(outs, axis=0).T.astype(o_ref.dtype)


def _moba(qt, kb, vt, kmean, eye, batch, seq, heads):
    t = kb.shape[0]
    blk = MOBA_BLOCK
    nb = seq // blk
    n_sel = max(1, min(MOBA_TOPK, nb - 1))
    width = heads * HEAD_DIM
    groups = ATTN_WIDTH // width
    kernel = functools.partial(_moba_kernel, scale=HEAD_DIM ** -0.5, n_sel=n_sel, heads=heads)
    acc_buf = pltpu.VMEM((LANES, blk), F32)
    score_buf = pltpu.VMEM((blk, blk), F32)
    return pl.pallas_call(
        kernel,
        grid=(batch, groups, nb),
        in_specs=[pl.BlockSpec((width, blk), lambda b, p, i: (p, b * nb + i)),
                  pl.BlockSpec((seq, width), lambda b, p, i: (b, p)),
                  pl.BlockSpec((width, seq), lambda b, p, i: (p, b)),
                  pl.BlockSpec((1, nb, width), lambda b, p, i: (b, 0, p)),
                  pl.BlockSpec((seq + blk, LANES), lambda b, p, i: (0, 0))],
        out_specs=pl.BlockSpec((blk, width), lambda b, p, i: (b * nb + i, p)),
        out_shape=jax.ShapeDtypeStruct((t, ATTN_WIDTH), BF16),
        scratch_shapes=[acc_buf] * heads + [score_buf] * (2 * heads),
        compiler_params=_params("parallel", "parallel", "arbitrary"),
        name="moba_attention",
    )(qt, kb, vt, kmean, eye)


def _conv_kernel(a_ref, g_ref, ah_ref, gh_ref, w_ref, b_ref, lg_ref, lb_ref, o_ref, u_scr,
                 *, ts):
    j = pl.program_id(1)
    halo = ah_ref[...] * jax.nn.sigmoid(gh_ref[...])
    u_scr[0, 0:CONV_HALO, :] = jnp.where(j > 0, halo, 0.0)
    u_scr[0, CONV_HALO:, :] = a_ref[...] * jax.nn.sigmoid(g_ref[...])
    rows = CONV_HALO + ts
    for r in range(1, SUBLANES):
        u_scr[r, 0:rows - SUBLANES, :] = u_scr[0, r:r + rows - SUBLANES, :]
    first = CONV_HALO - (CONV_KERNEL - 1)
    acc = jnp.zeros((ts, a_ref.shape[1]), F32)
    for tap in range(CONV_KERNEL):
        r = (first + tap) % SUBLANES
        base = first + tap - r
        acc = acc + w_ref[tap:tap + 1, :] * u_scr[r, base:base + ts, :]
    y = _layer_norm_rows(acc + b_ref[...], lg_ref[...], lb_ref[...])
    o_ref[...] = (y * jax.nn.sigmoid(y)).astype(o_ref.dtype)


def _conv(proj, a_col, g_col, w, b, ln_g, ln_b, batch, seq, ts):
    t = proj.shape[0]
    cw = w.shape[1]
    per_seq = seq // ts
    halo_per_tile = ts // CONV_HALO
    cur = lambda c: pl.BlockSpec((ts, cw), lambda bi, j, c=c: (bi * per_seq + j, c))
    halo = lambda c: pl.BlockSpec(
        (CONV_HALO, cw),
        lambda bi, j, c=c: (jnp.maximum((bi * per_seq + j) * halo_per_tile - 1, 0), c))
    vec = pl.BlockSpec((1, cw), lambda bi, j: (0, 0))
    return pl.pallas_call(
        functools.partial(_conv_kernel, ts=ts),
        grid=(batch, per_seq),
        in_specs=[cur(a_col), cur(g_col), halo(a_col), halo(g_col),
                  pl.BlockSpec((CONV_KERNEL, cw), lambda bi, j: (0, 0)), vec, vec, vec],
        out_specs=pl.BlockSpec((ts, cw), lambda bi, j: (bi * per_seq + j, 0)),
        out_shape=jax.ShapeDtypeStruct((t, cw), BF16),
        scratch_shapes=[pltpu.VMEM((SUBLANES, CONV_HALO + ts, cw), F32)],
        compiler_params=_params("parallel", "parallel"),
        name="conformer_conv",
    )(proj, proj, proj, proj, w, b.reshape(1, cw), ln_g.reshape(1, cw), ln_b.reshape(1, cw))


def _outproj_kernel(at_ref, cv_ref, x_ref, w_ref, g_ref, b_ref, o_ref, ot_ref, *, alpha):
    aw = at_ref.shape[1]
    mixed = jnp.dot(at_ref[...], w_ref[0:aw, :], preferred_element_type=F32)
    mixed = mixed + jnp.dot(cv_ref[...], w_ref[aw:, :], preferred_element_type=F32)
    y = _layer_norm_rows(alpha * x_ref[...] + mixed, g_ref[...], b_ref[...])
    o_ref[...] = y
    ot_ref[...] = y.T.astype(BF16)


def _outproj(attn, conv, x, w_out, g, b, alpha, tm):
    t, d = x.shape
    aw = attn.shape[1]
    cw = conv.shape[1]
    vec = pl.BlockSpec((1, d), lambda i: (0, 0))
    return pl.pallas_call(
        functools.partial(_outproj_kernel, alpha=alpha),
        grid=(t // tm,),
        in_specs=[pl.BlockSpec((tm, aw), lambda i: (i, 0)),
                  pl.BlockSpec((tm, cw), lambda i: (i, 0)),
                  pl.BlockSpec((tm, d), lambda i: (i, 0)),
                  pl.BlockSpec((aw + cw, d), lambda i: (0, 0)), vec, vec],
        out_specs=[pl.BlockSpec((tm, d), lambda i: (i, 0)),
                   pl.BlockSpec((d, tm), lambda i: (0, i))],
        out_shape=[jax.ShapeDtypeStruct((t, d), F32),
                   jax.ShapeDtypeStruct((d, t), BF16)],
        compiler_params=_params("parallel"),
        name="outproj_ln",
    )(attn, conv, x, w_out, g.reshape(1, d), b.reshape(1, d))


def _fold_kernel(k_ref, w_ref, o_ref):
    o_ref[...] = lax.dot_general(k_ref[0], w_ref[...], (((1,), (1,)), ((), ())),
                                 precision=lax.Precision.HIGHEST,
                                 preferred_element_type=F32).astype(o_ref.dtype)


def _fold_keys(keys, wq):
    d = wq.shape[0]
    groups = keys.shape[0]
    return pl.pallas_call(
        _fold_kernel,
        grid=(groups,),
        in_specs=[pl.BlockSpec((1, PEER_NKEYS, PEER_HALF), lambda i: (i, 0, 0)),
                  pl.BlockSpec((d, PEER_HALF), lambda i: (0, i))],
        out_specs=pl.BlockSpec((PEER_NKEYS, d), lambda i: (i, 0)),
        out_shape=jax.ShapeDtypeStruct((groups * PEER_NKEYS, d), BF16),
        compiler_params=_params("parallel"),
        name="peer_fold_keys",
    )(keys, wq)


assert PEER_TOPK == 2 * SUBLANES


def _batcher_pairs(n):
    pairs = []
    p = 1
    while p < n:
        k = p
        while k >= 1:
            for j in range(k % p, n - k, 2 * k):
                for i in range(min(k, n - j - k)):
                    if (i + j) // (2 * p) == (i + j + k) // (2 * p):
                        pairs.append((i + j, i + j + k))
            k //= 2
        p *= 2
    return pairs


def _top_values(tiles, k):
    n = 1
    while n < len(tiles):
        n *= 2
    tiles = list(tiles) + [None] * (n - len(tiles))
    for lo, hi in _batcher_pairs(n):
        a, b = tiles[lo], tiles[hi]
        if b is None:
            continue
        if a is None:
            tiles[lo], tiles[hi] = b, None
        else:
            tiles[lo], tiles[hi] = jnp.maximum(a, b), jnp.minimum(a, b)
    tiles = [t for t in tiles if t is not None]
    tops = []
    for _ in range(k):
        mx = jnp.max(tiles[0], axis=0, keepdims=True)
        hit = tiles[0] == mx
        for i in range(len(tiles) - 1):
            tiles[i] = jnp.where(hit, tiles[i + 1], tiles[i])
        tiles[-1] = jnp.where(hit, -jnp.inf, tiles[-1])
        tops.append(mx)
    return tops


def _row_tiles(x):
    return [x[i:i + SUBLANES, :] for i in range(0, x.shape[0], SUBLANES)]


def _route_kernel(s_ref, r2_ref, e2_ref, cnt_ref, e1_ref):
    nk = PEER_NKEYS
    k = PEER_TOPK
    sub = SUBLANES
    n = s_ref.shape[1]
    row = lax.broadcasted_iota(jnp.int32, (sub, n), 0)

    def head(h, carry):
        base = pl.multiple_of(h * 2 * nk, 2 * nk)
        s1 = s_ref[pl.ds(base, nk), :]
        s2 = s_ref[pl.ds(base + nk, nk), :]
        a = jnp.concatenate(_top_values(_row_tiles(s1), k), axis=0)
        b = jnp.concatenate(_top_values(_row_tiles(s2), k), axis=0)
        rank2 = jnp.full(s2.shape, float(k), F32)
        for r in range(k):
            rank2 = jnp.where(s2 == b[r:r + 1, :], float(r), rank2)
        cands = [a + b[0:1, :]]
        for jj in range(1, sub):
            keep = k // (jj + 1)
            v = a[0:sub, :] + b[jj:jj + 1, :]
            cands.append(v if keep >= sub else jnp.where(row < keep, v, -jnp.inf))
        cands.append(a[0:1, :] + b[sub:k, :])
        z = jnp.concatenate(cands, axis=0)
        best = _top_values(_row_tiles(z), k)
        chosen = z >= best[k - 1]
        zmax = best[0]
        denom = jnp.sum(jnp.where(chosen, jnp.exp(z - zmax), 0.0), axis=0, keepdims=True)
        ones = jnp.where(chosen, 1.0, 0.0)
        low = ones[0:sub, :]
        for jj in range(1, sub):
            low = low + ones[k + (jj - 1) * sub:k + jj * sub, :]
        tail = jnp.sum(ones[k + (sub - 1) * sub:, :], axis=0, keepdims=True)
        low = low + jnp.where(row == 0, tail, 0.0)
        cnt_sorted = jnp.concatenate([low, ones[sub:k, :]], axis=0)
        cnt = jnp.zeros(s1.shape, F32)
        for i in range(k):
            cnt = jnp.where(s1 == a[i:i + 1, :], cnt_sorted[i:i + 1, :], cnt)
        out = pl.multiple_of(h * nk, nk)
        r2_ref[pl.ds(out, nk), :] = rank2.astype(BF16)
        e2_ref[pl.ds(out, nk), :] = jnp.exp(s2 - b[0:1, :]).astype(BF16)
        cnt_ref[pl.ds(out, nk), :] = cnt
        e1_ref[pl.ds(out, nk), :] = jnp.exp(s1 - a[0:1, :]) * (1.0 / denom)
        return carry

    lax.fori_loop(0, PEER_HEADS, head, 0)


def _route(scores_t, tl):
    rows, t = scores_t.shape
    out_rows = rows // 2
    spec = pl.BlockSpec((out_rows, tl), lambda i: (0, i))
    shape = lambda dtype: jax.ShapeDtypeStruct((out_rows, t), dtype)
    return pl.pallas_call(
        _route_kernel,
        grid=(t // tl,),
        in_specs=[pl.BlockSpec((rows, tl), lambda i: (0, i))],
        out_specs=[spec, spec, spec, spec],
        out_shape=[shape(BF16), shape(BF16), shape(F32), shape(F32)],
        compiler_params=_params("parallel"),
        name="peer_route",
    )(scores_t)


_GATE_ROWS = 16
_GATE_LANES = 256


def _gate_block(ht_scr, a_scr, r2_ref, e2_ref, cnt_ref, e1_ref, key, c, lh):
    nk = PEER_NKEYS
    shape = (_GATE_ROWS, _GATE_LANES)
    ls = pl.ds(lh * _GATE_LANES, _GATE_LANES)
    cnt_b = [jnp.broadcast_to(cnt_ref[h, key:key + 1, ls], shape).astype(BF16)
             for h in range(PEER_HEADS)]
    e1_b = [jnp.broadcast_to(e1_ref[h, key:key + 1, ls], shape).astype(BF16)
            for h in range(PEER_HEADS)]
    for r in range(nk // _GATE_ROWS):
        gate = None
        for h in range(PEER_HEADS):
            rs = pl.ds(h * nk + r * _GATE_ROWS, _GATE_ROWS)
            e2 = e2_ref[rs, ls]
            term = jnp.where(r2_ref[rs, ls] < cnt_b[h], e2, jnp.zeros_like(e2)) * e1_b[h]
            gate = term if gate is None else gate + term
        hs = pl.ds(c * nk + r * _GATE_ROWS, _GATE_ROWS)
        hv = ht_scr[hs, ls].astype(BF16)
        half = hv * 0.5
        a_scr[hs, ls] = gate * (half + half * lax.erf(hv * (2.0 ** -0.5)))


def _peer_kernel(u_ref, xt_ref, vt_prev, vt_cur, r2_ref, e2_ref, cnt_prev, cnt_cur, e1_prev, e1_cur,
                 x_ref, g_ref, b_ref, o_ref, acc_scr, ht_a, ht_b, a_a, a_b, *, alpha, tiles):
    tn, tt = ht_a.shape
    groups = tn // PEER_NKEYS
    p = pl.program_id(1)

    @pl.when(p == 0)
    def _():
        acc_scr[...] = jnp.zeros_like(acc_scr)
        ht_b[...] = jnp.zeros_like(ht_b)
        a_a[...] = jnp.zeros_like(a_a)

    def sub_step(u_lo, vt_ref, vt_lo, a_in, ht_in, a_out, ht_out, cnt_ref, e1_ref, first_key):
        for lh in range(tt // _GATE_LANES):
            ls = pl.ds(lh * _GATE_LANES, _GATE_LANES)
            acc_scr[:, ls] += jnp.dot(vt_ref[:, vt_lo:vt_lo + tn], a_in[:, ls],
                                      preferred_element_type=F32)
            for c in range(groups):
                if c == groups // 2:
                    ht_out[:, ls] = jnp.dot(u_ref[u_lo:u_lo + tn, :], xt_ref[:, ls],
                                            preferred_element_type=F32)
                _gate_block(ht_in, a_out, r2_ref, e2_ref, cnt_ref, e1_ref, first_key + c, c, lh)

    def run(sub_steps):
        for t in sub_steps:
            a_in, a_out = (a_a, a_b) if t % 2 == 0 else (a_b, a_a)
            ht_out, ht_in = (ht_a, ht_b) if t % 2 == 0 else (ht_b, ht_a)
            vt_ref, vt_lo = (vt_prev, (tiles - 2 + t) * tn) if t < 2 else (vt_cur, (t - 2) * tn)
            cnt_ref, e1_ref, first_key = ((cnt_prev, e1_prev, (tiles - 1) * groups) if t == 0
                                          else (cnt_cur, e1_cur, (t - 1) * groups))
            sub_step(t * tn, vt_ref, vt_lo, a_in, ht_in, a_out, ht_out, cnt_ref, e1_ref, first_key)

    run(range(0, 2))

    @pl.when(p < pl.num_programs(1) - 1)
    def _():
        run(range(2, tiles))

    @pl.when(p == pl.num_programs(1) - 1)
    def _():
        y = alpha * x_ref[...] + acc_scr[...].T
        o_ref[...] = _layer_norm_rows(y, g_ref[...], b_ref[...])


def _peer(u_all, x_t, vt_all, layer, r2, e2, cnt, e1, x, g, b, alpha, tt, tn, tiles):
    _, n, d = u_all.shape
    t = x.shape[0]
    n_steps = n // (tiles * tn)
    groups = tn // PEER_NKEYS
    cnt3 = cnt.reshape(PEER_HEADS, PEER_NKEYS, t)
    e13 = e1.reshape(PEER_HEADS, PEER_NKEYS, t)
    this_blk = lambda p: jnp.minimum(p, n_steps - 1)
    last_blk = lambda p: jnp.maximum(p - 1, 0)
    tok = pl.BlockSpec((PEER_HEADS * PEER_NKEYS, tt), lambda j, p: (0, j))
    keys = lambda which: pl.BlockSpec((PEER_HEADS, tiles * groups, tt),
                                      lambda j, p: (0, which(p), j))
    vts = lambda which: pl.BlockSpec((None, d, tiles * tn), lambda j, p: (layer, 0, which(p)))
    vec = pl.BlockSpec((1, d), lambda j, p: (0, 0))
    return pl.pallas_call(
        functools.partial(_peer_kernel, alpha=alpha, tiles=tiles),
        grid=(t // tt, n_steps + 1),
        in_specs=[pl.BlockSpec((None, tiles * tn, d), lambda j, p: (layer, this_blk(p), 0)),
                  pl.BlockSpec((d, tt), lambda j, p: (0, j)),
                  vts(last_blk), vts(this_blk),
                  tok, tok, keys(last_blk), keys(this_blk), keys(last_blk), keys(this_blk),
                  pl.BlockSpec((tt, d), lambda j, p: (j, 0)), vec, vec],
        out_specs=pl.BlockSpec((tt, d), lambda j, p: (j, 0)),
        out_shape=jax.ShapeDtypeStruct((t, d), F32),
        scratch_shapes=[pltpu.VMEM((d, tt), F32),
                        pltpu.VMEM((tn, tt), F32), pltpu.VMEM((tn, tt), F32),
                        pltpu.VMEM((tn, tt), BF16), pltpu.VMEM((tn, tt), BF16)],
        compiler_params=_params("parallel", "arbitrary"),
        name="peer_experts",
    )(u_all, x_t, vt_all, vt_all, r2, e2, cnt3, cnt3, e13, e13, x, g.reshape(1, d), b.reshape(1, d))


def _rope_tables(seq):
    pos = jnp.arange(seq, dtype=F32)
    inv = jnp.power(ROPE_THETA, -jnp.arange(0, HEAD_DIM, 2, dtype=F32) / HEAD_DIM)
    ang = pos[:, None] * inv[None, :]
    cos = jnp.tile(jnp.cos(ang), (1, 2 * ATTN_HEADS))
    sin = jnp.tile(jnp.sin(ang), (1, 2 * ATTN_HEADS))
    return cos, sin


def kernel(x, w_in, conv_w, conv_b, conv_ln_g, conv_ln_b, w_out, ln1_g, ln1_b,
           peer_wq, peer_keys, peer_u, peer_v, ln2_g, ln2_b):
    batch, seq, d = x.shape
    depth = w_in.shape[0]
    t = batch * seq
    aw = ATTN_WIDTH
    alpha = (2 * depth) ** 0.25
    nb = seq // MOBA_BLOCK

    cos, sin = _rope_tables(seq)
    eye = (jnp.arange(seq + MOBA_BLOCK)[:, None] // MOBA_BLOCK
           == jnp.arange(LANES)[None, :]).astype(BF16)

    u_all = peer_u.astype(BF16)
    vt_all = jnp.swapaxes(peer_v.astype(BF16), 1, 2)

    xf = x.reshape(t, d)
    for l in range(depth):
        qt, kb, vt, kmean, ag = _inproj(xf, w_in[l].astype(BF16), cos, sin, seq)
        attn = _moba(qt, kb, vt, kmean.reshape(batch, nb, aw), eye, batch, seq, heads=4)
        conv = _conv(ag, 0, 1, conv_w[l], conv_b[l], conv_ln_g[l], conv_ln_b[l],
                     batch, seq, ts=512)
        x1, x1t = _outproj(attn, conv, xf, w_out[l].astype(BF16), ln1_g[l], ln1_b[l], alpha, tm=256)

        groups = PEER_HEADS * 2
        ws_t = _fold_keys(peer_keys[l].reshape(groups, PEER_NKEYS, PEER_HALF), peer_wq[l])
        scores_t = _matmul(ws_t, x1t, tm=ws_t.shape[0], tn=512, out_dtype=F32, name="peer_scores")
        r2, e2, cnt, e1 = _route(scores_t, tl=512)
        xf = _peer(u_all, x1t, vt_all, l, r2, e2, cnt, e1, x1, ln2_g[l], ln2_b[l], alpha,
                   tt=512, tn=512, tiles=4)
    return xf.reshape(batch, seq, d)
```

```python
import functools

import jax
import jax.numpy as jnp
from jax import lax
from jax.experimental import pallas as pl
from jax.experimental.pallas import tpu as pltpu

ATTN_HEADS = 8
HEAD_DIM = 64
ATTN_WIDTH = ATTN_HEADS * HEAD_DIM
CONV_KERNEL = 31
ROPE_THETA = 10000.0
MOBA_BLOCK = 256
MOBA_TOPK = 3
PEER_HEADS = 8
PEER_NKEYS = 128
PEER_HALF = 128
PEER_TOPK = 16
LN_EPS = 1e-5

LANES = 128
SUBLANES = 8
VMEM_LIMIT_BYTES = 56 * 1024 * 1024

NEG_BIG = -1e30
CONV_HALO = 32

F32 = jnp.float32
BF16 = jnp.bfloat16


def _params(*semantics):
    return pltpu.CompilerParams(dimension_semantics=semantics,
                                vmem_limit_bytes=VMEM_LIMIT_BYTES)


def _layer_norm_rows(y, g, b):
    mu = jnp.mean(y, axis=-1, keepdims=True)
    yc = y - mu
    var = jnp.mean(yc * yc, axis=-1, keepdims=True)
    return yc * lax.rsqrt(var + LN_EPS) * g + b


def _rotate_half(x):
    n = x.shape[1]
    half = HEAD_DIM // 2
    col = lax.broadcasted_iota(jnp.int32, x.shape, 1)
    from_above = pltpu.roll(x, n - half, 1)
    from_below = pltpu.roll(x, half, 1)
    return jnp.where(col % HEAD_DIM < half, -from_above, from_below)


def _inproj_kernel(x_ref, w_ref, cos_ref, sin_ref, qt_ref, ko_ref, vt_ref, km_ref, ag_ref):
    w = ATTN_WIDTH
    proj = jnp.dot(x_ref[...].astype(BF16), w_ref[...], preferred_element_type=F32)
    cos = cos_ref[...]
    sin = sin_ref[...]
    q = proj[:, 0:w]
    k = proj[:, w:2 * w]
    qt_ref[...] = (q * cos + _rotate_half(q) * sin).T
    k = k * cos + _rotate_half(k) * sin
    ko_ref[...] = k.astype(BF16)
    vt_ref[...] = proj[:, 2 * w:3 * w].T.astype(BF16)
    km_ref[0] = jnp.mean(k, axis=0, keepdims=True)
    ag_ref[...] = proj[:, 3 * w:]


def _inproj(x, w_ext, cos, sin, seq):
    t, d = x.shape
    w = ATTN_WIDTH
    n = w_ext.shape[1]
    nblk = t // MOBA_BLOCK
    per_seq = seq // MOBA_BLOCK
    tab = pl.BlockSpec((MOBA_BLOCK, w), lambda i: (i % per_seq, 0))
    colmajor = pl.BlockSpec((w, MOBA_BLOCK), lambda i: (0, i))
    return pl.pallas_call(
        _inproj_kernel,
        grid=(nblk,),
        in_specs=[pl.BlockSpec((MOBA_BLOCK, d), lambda i: (i, 0)),
                  pl.BlockSpec((d, n), lambda i: (0, 0)), tab, tab],
        out_specs=[colmajor, pl.BlockSpec((MOBA_BLOCK, w), lambda i: (i, 0)), colmajor,
                   pl.BlockSpec((1, 1, w), lambda i: (i, 0, 0)),
                   pl.BlockSpec((MOBA_BLOCK, n - 3 * w), lambda i: (i, 0))],
        out_shape=[jax.ShapeDtypeStruct((w, t), F32),
                   jax.ShapeDtypeStruct((t, w), BF16),
                   jax.ShapeDtypeStruct((w, t), BF16),
                   jax.ShapeDtypeStruct((nblk, 1, w), F32),
                   jax.ShapeDtypeStruct((t, n - 3 * w), F32)],
        compiler_params=_params("parallel"),
        name="in_proj_rope",
    )(x, w_ext, cos, sin)


def _tree_reduce(op, s):
    r = s.shape[0]
    while r > SUBLANES:
        r //= 2
        s = op(s[0:r, :], s[r:2 * r, :])
    if op is jnp.add:
        return jnp.sum(s, axis=0, keepdims=True)
    return jnp.max(s, axis=0, keepdims=True)


def _moba_kernel(qt_ref, k_ref, vt_ref, km_ref, e_ref, o_ref, *scratch, scale, n_sel, heads):
    blk = MOBA_BLOCK
    nb = km_ref.shape[1]
    seq = k_ref.shape[0]
    i = pl.program_id(2)
    accs = scratch[0:heads]
    slots = (scratch[heads:2 * heads], scratch[2 * heads:3 * heads])
    chan = lax.broadcasted_iota(jnp.int32, (LANES, blk), 0)
    km_chan = lax.broadcasted_iota(jnp.int32, (nb, LANES), 1)
    blk_ix = lax.broadcasted_iota(jnp.int32, (nb, blk), 0)
    pad_ix = lax.broadcasted_iota(jnp.int32, (LANES - nb, blk), 0)

    def pair_rows(h):
        return slice((h // 2) * LANES, (h // 2 + 1) * LANES)

    qexts = []
    for h in range(heads):
        qt = qt_ref[pair_rows(h), :]
        km = km_ref[0, :, pair_rows(h)]
        kmh = jnp.where(km_chan // HEAD_DIM == h % 2, km, 0.0)
        gate = jnp.dot(kmh, qt, precision=lax.Precision.HIGHEST,
                       preferred_element_type=F32)
        g = jnp.where(blk_ix < i, gate, -jnp.inf)
        bias = jnp.full((nb, blk), NEG_BIG, F32)
        for _ in range(n_sel):
            mx = jnp.max(g, axis=0, keepdims=True)
            first = jnp.min(jnp.where(g == mx, blk_ix, nb), axis=0, keepdims=True)
            first = jnp.where(mx > -jnp.inf, first, nb)
            pick = blk_ix == first
            bias = jnp.where(pick, 0.0, bias)
            g = jnp.where(pick, -jnp.inf, g)
        qh = (jnp.where(chan // HEAD_DIM == h % 2, qt, 0.0) * scale).astype(BF16)
        pad = jnp.where(pad_ix == 0, NEG_BIG, 0.0).astype(BF16)
        qexts.append(jnp.concatenate([qh, bias.astype(BF16), pad], axis=0))

    own = pl.multiple_of(i * blk, blk)
    key_ix = lax.broadcasted_iota(jnp.int32, (blk, blk), 0)
    qry_ix = lax.broadcasted_iota(jnp.int32, (blk, blk), 1)
    ss = [jnp.dot(k_ref[pl.ds(own, blk), pair_rows(h)], qexts[h][0:LANES, :],
                  preferred_element_type=F32) for h in range(heads)]
    stats = []
    for h in range(heads):
        s = jnp.where(key_ix <= qry_ix, ss[h], NEG_BIG)
        m = _tree_reduce(jnp.maximum, s)
        p = jnp.exp(s - m)
        stats += [m, _tree_reduce(jnp.add, p)]
        accs[h][...] = jnp.dot(vt_ref[pair_rows(h), pl.ds(own, blk)], p.astype(BF16),
                               preferred_element_type=F32)

    def key_offset(j):
        return pl.multiple_of(jnp.where(j < i, j, 0) * blk, blk)

    def scores(j):
        e_off = pl.multiple_of(jnp.where(j < i, j * blk, seq), blk)
        e_blk = e_ref[pl.ds(e_off, blk), :]
        kexts = [jnp.concatenate([k_ref[pl.ds(key_offset(j), blk), pair_rows(2 * pr)], e_blk], axis=1)
                 for pr in range(heads // 2)]
        return [jnp.dot(kexts[h // 2], qexts[h], preferred_element_type=F32) for h in range(heads)]

    def consume(slot, j, carry):
        out = []
        for h in range(heads):
            m_prev, l_prev = carry[2 * h], carry[2 * h + 1]
            s = slot[h][...]
            m_new = jnp.maximum(m_prev, _tree_reduce(jnp.maximum, s))
            alpha = jnp.exp(m_prev - m_new)
            p = jnp.exp(s - m_new)
            vj = vt_ref[pair_rows(h), pl.ds(key_offset(j), blk)]
            accs[h][...] = alpha * accs[h][...] + jnp.dot(vj, p.astype(BF16),
                                                          preferred_element_type=F32)
            out += [m_new, alpha * l_prev + _tree_reduce(jnp.add, p)]
        return tuple(out)

    def fill(slot, ss):
        for h in range(heads):
            slot[h][...] = ss[h]

    fill(slots[0], scores(0))

    def body(jj, carry):
        j0 = 2 * jj
        nxt = scores(j0 + 1)
        carry = consume(slots[0], j0, carry)
        fill(slots[1], nxt)
        nxt = scores(j0 + 2)
        carry = consume(slots[1], j0 + 1, carry)
        fill(slots[0], nxt)
        return carry

    stats = lax.fori_loop(0, (i + 1) // 2, body, tuple(stats))
    outs = []
    for h in range(heads):
        lo = (h % 2) * HEAD_DIM
        outs.append(accs[h][lo:lo + HEAD_DIM, :] / stats[2 * h + 1])
    o_ref[...] = jnp.concatenate(outs, axis=0).T.astype(o_ref.dtype)


def _moba(qt, kb, vt, kmean, eye, batch, seq, heads):
    t = kb.shape[0]
    blk = MOBA_BLOCK
    nb = seq // blk
    n_sel = max(1, min(MOBA_TOPK, nb - 1))
    width = heads * HEAD_DIM
    groups = ATTN_WIDTH // width
    kernel = functools.partial(_moba_kernel, scale=HEAD_DIM ** -0.5, n_sel=n_sel, heads=heads)
    acc_buf = pltpu.VMEM((LANES, blk), F32)
    score_buf = pltpu.VMEM((blk, blk), F32)
    return pl.pallas_call(
        kernel,
        grid=(batch, groups, nb),
        in_specs=[pl.BlockSpec((width, blk), lambda b, p, i: (p, b * nb + i)),
                  pl.BlockSpec((seq, width), lambda b, p, i: (b, p)),
                  pl.BlockSpec((width, seq), lambda b, p, i: (p, b)),
                  pl.BlockSpec((1, nb, width), lambda b, p, i: (b, 0, p)),
                  pl.BlockSpec((seq + blk, LANES), lambda b, p, i: (0, 0))],
        out_specs=pl.BlockSpec((blk, width), lambda b, p, i: (b * nb + i, p)),
        out_shape=jax.ShapeDtypeStruct((t, ATTN_WIDTH), BF16),
        scratch_shapes=[acc_buf] * heads + [score_buf] * (2 * heads),
        compiler_params=_params("parallel", "parallel", "arbitrary"),
        name="moba_attention",
    )(qt, kb, vt, kmean, eye)


def _conv_kernel(a_ref, g_ref, ah_ref, gh_ref, w_ref, b_ref, lg_ref, lb_ref, o_ref, u_scr,
                 *, ts):
    j = pl.program_id(1)
    halo = ah_ref[...] * jax.nn.sigmoid(gh_ref[...])
    u_scr[0, 0:CONV_HALO, :] = jnp.where(j > 0, halo, 0.0)
    u_scr[0, CONV_HALO:, :] = a_ref[...] * jax.nn.sigmoid(g_ref[...])
    rows = CONV_HALO + ts
    for r in range(1, SUBLANES):
        u_scr[r, 0:rows - SUBLANES, :] = u_scr[0, r:r + rows - SUBLANES, :]
    first = CONV_HALO - (CONV_KERNEL - 1)
    acc = jnp.zeros((ts, a_ref.shape[1]), F32)
    for tap in range(CONV_KERNEL):
        r = (first + tap) % SUBLANES
        base = first + tap - r
        acc = acc + w_ref[tap:tap + 1, :] * u_scr[r, base:base + ts, :]
    y = _layer_norm_rows(acc + b_ref[...], lg_ref[...], lb_ref[...])
    o_ref[...] = (y * jax.nn.sigmoid(y)).astype(o_ref.dtype)


def _conv(proj, a_col, g_col, w, b, ln_g, ln_b, batch, seq, ts):
    t = proj.shape[0]
    cw = w.shape[1]
    per_seq = seq // ts
    halo_per_tile = ts // CONV_HALO
    cur = lambda c: pl.BlockSpec((ts, cw), lambda bi, j, c=c: (bi * per_seq + j, c))
    halo = lambda c: pl.BlockSpec(
        (CONV_HALO, cw),
        lambda bi, j, c=c: (jnp.maximum((bi * per_seq + j) * halo_per_tile - 1, 0), c))
    vec = pl.BlockSpec((1, cw), lambda bi, j: (0, 0))
    return pl.pallas_call(
        functools.partial(_conv_kernel, ts=ts),
        grid=(batch, per_seq),
        in_specs=[cur(a_col), cur(g_col), halo(a_col), halo(g_col),
                  pl.BlockSpec((CONV_KERNEL, cw), lambda bi, j: (0, 0)), vec, vec, vec],
        out_specs=pl.BlockSpec((ts, cw), lambda bi, j: (bi * per_seq + j, 0)),
        out_shape=jax.ShapeDtypeStruct((t, cw), BF16),
        scratch_shapes=[pltpu.VMEM((SUBLANES, CONV_HALO + ts, cw), F32)],
        compiler_params=_params("parallel", "parallel"),
        name="conformer_conv",
    )(proj, proj, proj, proj, w, b.reshape(1, cw), ln_g.reshape(1, cw), ln_b.reshape(1, cw))


def _outproj_kernel(at_ref, cv_ref, x_ref, w_ref, g_ref, b_ref, ws_ref, o_ref, ot_ref, s_ref,
                    *, alpha):
    aw = at_ref.shape[1]
    mixed = jnp.dot(at_ref[...], w_ref[0:aw, :], preferred_element_type=F32)
    mixed = mixed + jnp.dot(cv_ref[...], w_ref[aw:, :], preferred_element_type=F32)
    y = _layer_norm_rows(alpha * x_ref[...] + mixed, g_ref[...], b_ref[...])
    o_ref[...] = y
    yt = y.T.astype(BF16)
    ot_ref[...] = yt
    s_ref[...] = jnp.dot(ws_ref[...], yt, preferred_element_type=F32)


def _outproj(attn, conv, x, w_out, g, b, ws_t, alpha, tm):
    t, d = x.shape
    aw = attn.shape[1]
    cw = conv.shape[1]
    rows = ws_t.shape[0]
    vec = pl.BlockSpec((1, d), lambda i: (0, 0))
    return pl.pallas_call(
        functools.partial(_outproj_kernel, alpha=alpha),
        grid=(t // tm,),
        in_specs=[pl.BlockSpec((tm, aw), lambda i: (i, 0)),
                  pl.BlockSpec((tm, cw), lambda i: (i, 0)),
                  pl.BlockSpec((tm, d), lambda i: (i, 0)),
                  pl.BlockSpec((aw + cw, d), lambda i: (0, 0)), vec, vec,
                  pl.BlockSpec((rows, d), lambda i: (0, 0))],
        out_specs=[pl.BlockSpec((tm, d), lambda i: (i, 0)),
                   pl.BlockSpec((d, tm), lambda i: (0, i)),
                   pl.BlockSpec((rows, tm), lambda i: (0, i))],
        out_shape=[jax.ShapeDtypeStruct((t, d), F32),
                   jax.ShapeDtypeStruct((d, t), BF16),
                   jax.ShapeDtypeStruct((rows, t), F32)],
        compiler_params=_params("parallel"),
        name="outproj_ln_scores",
    )(attn, conv, x, w_out, g.reshape(1, d), b.reshape(1, d), ws_t)


def _fold_kernel(k_ref, w_ref, o_ref):
    o_ref[...] = lax.dot_general(k_ref[0], w_ref[...], (((1,), (1,)), ((), ())),
                                 precision=lax.Precision.HIGHEST,
                                 preferred_element_type=F32).astype(o_ref.dtype)


def _fold_keys(keys, wq):
    d = wq.shape[0]
    groups = keys.shape[0]
    return pl.pallas_call(
        _fold_kernel,
        grid=(groups,),
        in_specs=[pl.BlockSpec((1, PEER_NKEYS, PEER_HALF), lambda i: (i, 0, 0)),
                  pl.BlockSpec((d, PEER_HALF), lambda i: (0, i))],
        out_specs=pl.BlockSpec((PEER_NKEYS, d), lambda i: (i, 0)),
        out_shape=jax.ShapeDtypeStruct((groups * PEER_NKEYS, d), BF16),
        compiler_params=_params("parallel"),
        name="peer_fold_keys",
    )(keys, wq)


assert PEER_TOPK == 2 * SUBLANES


def _batcher_pairs(n):
    pairs = []
    p = 1
    while p < n:
        k = p
        while k >= 1:
            for j in range(k % p, n - k, 2 * k):
                for i in range(min(k, n - j - k)):
                    if (i + j) // (2 * p) == (i + j + k) // (2 * p):
                        pairs.append((i + j, i + j + k))
            k //= 2
        p *= 2
    return pairs


def _top_values(tiles, k):
    n = 1
    while n < len(tiles):
        n *= 2
    tiles = list(tiles) + [None] * (n - len(tiles))
    for lo, hi in _batcher_pairs(n):
        a, b = tiles[lo], tiles[hi]
        if b is None:
            continue
        if a is None:
            tiles[lo], tiles[hi] = b, None
        else:
            tiles[lo], tiles[hi] = jnp.maximum(a, b), jnp.minimum(a, b)
    tiles = [t for t in tiles if t is not None]
    sub_ix = lax.broadcasted_iota(jnp.int32, tiles[0].shape, 0)
    tops = []
    for _ in range(k):
        mx = jnp.max(tiles[0], axis=0, keepdims=True)
        hit_ix = jnp.where(tiles[0] == mx, sub_ix, SUBLANES)
        hit = hit_ix == jnp.min(hit_ix, axis=0, keepdims=True)
        for i in range(len(tiles) - 1):
            tiles[i] = jnp.where(hit, tiles[i + 1], tiles[i])
        tiles[-1] = jnp.where(hit, -jnp.inf, tiles[-1])
        tops.append(mx)
    return tops


def _assign_by_value(s, tops, values, default, row_ix):
    out = jnp.full(s.shape, default, F32)
    cur = s
    for r in range(tops.shape[0]):
        val = float(r) if values is None else values[r:r + 1, :]
        first = jnp.min(jnp.where(cur == tops[r:r + 1, :], row_ix, s.shape[0]), axis=0, keepdims=True)
        pick = row_ix == first
        out = jnp.where(pick, val, out)
        cur = jnp.where(pick, -jnp.inf, cur)
    return out


def _row_tiles(x):
    return [x[i:i + SUBLANES, :] for i in range(0, x.shape[0], SUBLANES)]


def _route_kernel(s_ref, r2_ref, e2_ref, cnt_ref, e1_ref):
    nk = PEER_NKEYS
    k = PEER_TOPK
    sub = SUBLANES
    n = s_ref.shape[1]
    row = lax.broadcasted_iota(jnp.int32, (sub, n), 0)
    row16 = lax.broadcasted_iota(jnp.int32, (k, n), 0)
    row128 = lax.broadcasted_iota(jnp.int32, (nk, n), 0)

    def candidates(a, b):
        cands = [a + b[0:1, :]]
        combs = [row16 * k]
        for jj in range(1, sub):
            keep = k // (jj + 1)
            v = a[0:sub, :] + b[jj:jj + 1, :]
            cands.append(v if keep >= sub else jnp.where(row < keep, v, -jnp.inf))
            combs.append(row * k + jj)
        cands.append(a[0:1, :] + b[sub:k, :])
        combs.append(row + sub)
        return jnp.concatenate(cands, axis=0), jnp.concatenate(combs, axis=0)

    def counts_per_first_index(ones):
        low = ones[0:sub, :]
        for jj in range(1, sub):
            low = low + ones[k + (jj - 1) * sub:k + jj * sub, :]
        tail = jnp.sum(ones[k + (sub - 1) * sub:, :], axis=0, keepdims=True)
        low = low + jnp.where(row == 0, tail, 0.0)
        return jnp.concatenate([low, ones[sub:k, :]], axis=0)

    def store(h, s1, s2, a, b, z, ones, rank2, cnt):
        zmax = a[0:1, :] + b[0:1, :]
        denom = jnp.sum(ones * jnp.exp(z - zmax), axis=0, keepdims=True)
        out = pl.multiple_of(h * nk, nk)
        r2_ref[pl.ds(out, nk), :] = rank2.astype(BF16)
        e2_ref[pl.ds(out, nk), :] = jnp.exp(s2 - b[0:1, :]).astype(BF16)
        cnt_ref[pl.ds(out, nk), :] = cnt
        e1_ref[pl.ds(out, nk), :] = jnp.exp(s1 - a[0:1, :]) * (1.0 / denom)

    def head(h, carry):
        base = pl.multiple_of(h * 2 * nk, 2 * nk)
        s1 = s_ref[pl.ds(base, nk), :]
        s2 = s_ref[pl.ds(base + nk, nk), :]
        a = jnp.concatenate(_top_values(_row_tiles(s1), k), axis=0)
        b = jnp.concatenate(_top_values(_row_tiles(s2), k), axis=0)
        z, comb = candidates(a, b)
        best = jnp.concatenate(_top_values(_row_tiles(z), k), axis=0)

        ones = jnp.where(z >= best[k - 1:k, :], 1.0, 0.0)
        cnt_sorted = counts_per_first_index(ones)
        rank2 = jnp.full(s2.shape, float(k), F32)
        cnt = jnp.zeros(s1.shape, F32)
        for r in range(k):
            rank2 = jnp.where(s2 == b[r:r + 1, :], float(r), rank2)
            cnt = jnp.where(s1 == a[r:r + 1, :], cnt_sorted[r:r + 1, :], cnt)
        store(h, s1, s2, a, b, z, ones, rank2, cnt)

        def excess(x, kth):
            return jnp.sum(jnp.where(x >= kth, 1.0, 0.0), axis=0, keepdims=True) - float(k)

        def equal_neighbours(lst):
            return jnp.sum(jnp.where(lst[0:k - 1, :] == lst[1:k, :], 1.0, 0.0), axis=0, keepdims=True)

        ties = (excess(s1, a[k - 1:k, :]) + excess(s2, b[k - 1:k, :]) + excess(z, best[k - 1:k, :])
                + equal_neighbours(a) + equal_neighbours(b) + equal_neighbours(best))

        @pl.when(jnp.max(ties) > 0.0)
        def _():
            cur = z
            ones_x = jnp.zeros(z.shape, F32)
            for _ in range(k):
                mx = jnp.max(cur, axis=0, keepdims=True)
                first = jnp.min(jnp.where(cur == mx, comb, k * k), axis=0, keepdims=True)
                pick = comb == first
                ones_x = jnp.where(pick, 1.0, ones_x)
                cur = jnp.where(pick, -jnp.inf, cur)
            rank2_x = _assign_by_value(s2, b, None, float(k), row128)
            cnt_x = _assign_by_value(s1, a, counts_per_first_index(ones_x), 0.0, row128)
            store(h, s1, s2, a, b, z, ones_x, rank2_x, cnt_x)

        return carry

    lax.fori_loop(0, PEER_HEADS, head, 0)


def _route(scores_t, tl):
    rows, t = scores_t.shape
    out_rows = rows // 2
    spec = pl.BlockSpec((out_rows, tl), lambda i: (0, i))
    shape = lambda dtype: jax.ShapeDtypeStruct((out_rows, t), dtype)
    return pl.pallas_call(
        _route_kernel,
        grid=(t // tl,),
        in_specs=[pl.BlockSpec((rows, tl), lambda i: (0, i))],
        out_specs=[spec, spec, spec, spec],
        out_shape=[shape(BF16), shape(BF16), shape(F32), shape(F32)],
        compiler_params=_params("parallel"),
        name="peer_route",
    )(scores_t)


_GATE_ROWS = 16
_GATE_LANES = 256


def _gate_block(ht_scr, a_scr, r2_ref, e2_ref, cnt_ref, e1_ref, key, c, lh):
    nk = PEER_NKEYS
    shape = (_GATE_ROWS, _GATE_LANES)
    ls = pl.ds(lh * _GATE_LANES, _GATE_LANES)
    cnt_b = [jnp.broadcast_to(cnt_ref[h, key:key + 1, ls], shape).astype(BF16)
             for h in range(PEER_HEADS)]
    e1_b = [jnp.broadcast_to(e1_ref[h, key:key + 1, ls], shape).astype(BF16)
            for h in range(PEER_HEADS)]
    for r in range(nk // _GATE_ROWS):
        gate = None
        for h in range(PEER_HEADS):
            rs = pl.ds(h * nk + r * _GATE_ROWS, _GATE_ROWS)
            e2 = e2_ref[rs, ls]
            term = jnp.where(r2_ref[rs, ls] < cnt_b[h], e2, jnp.zeros_like(e2)) * e1_b[h]
            gate = term if gate is None else gate + term
        hs = pl.ds(c * nk + r * _GATE_ROWS, _GATE_ROWS)
        hv = ht_scr[hs, ls].astype(BF16)
        half = hv * 0.5
        a_scr[hs, ls] = gate * (half + half * lax.erf(hv * (2.0 ** -0.5)))


def _peer_kernel(u_ref, xt_ref, vt_ref, r2_ref, e2_ref, cnt_prev, cnt_cur, e1_prev, e1_cur,
                 x_ref, g_ref, b_ref, o_ref, acc_scr, ht_a, ht_b, a_a, a_b, *, alpha):
    tn, tt = ht_a.shape
    groups = tn // PEER_NKEYS
    p = pl.program_id(1)

    @pl.when(p == 0)
    def _():
        acc_scr[...] = jnp.zeros_like(acc_scr)
        ht_b[...] = jnp.zeros_like(ht_b)
        a_a[...] = jnp.zeros_like(a_a)

    def half_step(u_lo, a_in, ht_in, a_out, ht_out, cnt_ref, e1_ref, first_key):
        for lh in range(tt // _GATE_LANES):
            ls = pl.ds(lh * _GATE_LANES, _GATE_LANES)
            acc_scr[:, ls] += jnp.dot(vt_ref[:, u_lo:u_lo + tn], a_in[:, ls],
                                      preferred_element_type=F32)
            for c in range(groups):
                if c == groups // 2:
                    ht_out[:, ls] = jnp.dot(u_ref[u_lo:u_lo + tn, :], xt_ref[:, ls],
                                            preferred_element_type=F32)
                _gate_block(ht_in, a_out, r2_ref, e2_ref, cnt_ref, e1_ref, first_key + c, c, lh)

    half_step(0, a_a, ht_b, a_b, ht_a, cnt_prev, e1_prev, groups)
    half_step(tn, a_b, ht_a, a_a, ht_b, cnt_cur, e1_cur, 0)

    @pl.when(p == pl.num_programs(1) - 1)
    def _():
        y = alpha * x_ref[...] + acc_scr[...].T
        o_ref[...] = _layer_norm_rows(y, g_ref[...], b_ref[...])


def _peer(u_all, x_t, vt_all, layer, r2, e2, cnt, e1, x, g, b, alpha, tt, tn):
    _, n, d = u_all.shape
    t = x.shape[0]
    n_pairs = n // (2 * tn)
    groups = tn // PEER_NKEYS
    cnt3 = cnt.reshape(PEER_HEADS, PEER_NKEYS, t)
    e13 = e1.reshape(PEER_HEADS, PEER_NKEYS, t)
    this_pair = lambda p: jnp.minimum(p, n_pairs - 1)
    last_pair = lambda p: jnp.maximum(p - 1, 0)
    tok = pl.BlockSpec((PEER_HEADS * PEER_NKEYS, tt), lambda j, p: (0, j))
    keys = lambda which: pl.BlockSpec((PEER_HEADS, 2 * groups, tt), lambda j, p: (0, which(p), j))
    vec = pl.BlockSpec((1, d), lambda j, p: (0, 0))
    return pl.pallas_call(
        functools.partial(_peer_kernel, alpha=alpha),
        grid=(t // tt, n_pairs + 1),
        in_specs=[pl.BlockSpec((None, 2 * tn, d), lambda j, p: (layer, this_pair(p), 0)),
                  pl.BlockSpec((d, tt), lambda j, p: (0, j)),
                  pl.BlockSpec((None, d, 2 * tn), lambda j, p: (layer, 0, last_pair(p))),
                  tok, tok, keys(last_pair), keys(this_pair), keys(last_pair), keys(this_pair),
                  pl.BlockSpec((tt, d), lambda j, p: (j, 0)), vec, vec],
        out_specs=pl.BlockSpec((tt, d), lambda j, p: (j, 0)),
        out_shape=jax.ShapeDtypeStruct((t, d), F32),
        scratch_shapes=[pltpu.VMEM((d, tt), F32),
                        pltpu.VMEM((tn, tt), F32), pltpu.VMEM((tn, tt), F32),
                        pltpu.VMEM((tn, tt), BF16), pltpu.VMEM((tn, tt), BF16)],
        compiler_params=_params("parallel", "arbitrary"),
        name="peer_experts",
    )(u_all, x_t, vt_all, r2, e2, cnt3, cnt3, e13, e13, x, g.reshape(1, d), b.reshape(1, d))


def _rope_tables(seq):
    pos = jnp.arange(seq, dtype=F32)
    inv = jnp.power(ROPE_THETA, -jnp.arange(0, HEAD_DIM, 2, dtype=F32) / HEAD_DIM)
    ang = pos[:, None] * inv[None, :]
    cos = jnp.tile(jnp.cos(ang), (1, 2 * ATTN_HEADS))
    sin = jnp.tile(jnp.sin(ang), (1, 2 * ATTN_HEADS))
    return cos, sin


def kernel(x, w_in, conv_w, conv_b, conv_ln_g, conv_ln_b, w_out, ln1_g, ln1_b,
           peer_wq, peer_keys, peer_u, peer_v, ln2_g, ln2_b):
    batch, seq, d = x.shape
    depth = w_in.shape[0]
    t = batch * seq
    aw = ATTN_WIDTH
    alpha = (2 * depth) ** 0.25
    nb = seq // MOBA_BLOCK

    cos, sin = _rope_tables(seq)
    eye = (jnp.arange(seq + MOBA_BLOCK)[:, None] // MOBA_BLOCK
           == jnp.arange(LANES)[None, :]).astype(BF16)

    u_all = peer_u.astype(BF16)
    vt_all = jnp.swapaxes(peer_v.astype(BF16), 1, 2)

    xf = x.reshape(t, d)
    for l in range(depth):
        qt, kb, vt, kmean, ag = _inproj(xf, w_in[l].astype(BF16), cos, sin, seq)
        attn = _moba(qt, kb, vt, kmean.reshape(batch, nb, aw), eye, batch, seq, heads=4)
        conv = _conv(ag, 0, 1, conv_w[l], conv_b[l], conv_ln_g[l], conv_ln_b[l],
                     batch, seq, ts=512)
        groups = PEER_HEADS * 2
        ws_t = _fold_keys(peer_keys[l].reshape(groups, PEER_NKEYS, PEER_HALF), peer_wq[l])
        x1, x1t, scores_t = _outproj(attn, conv, xf, w_out[l].astype(BF16), ln1_g[l], ln1_b[l],
                                     ws_t, alpha, tm=256)
        r2, e2, cnt, e1 = _route(scores_t, tl=512)
        xf = _peer(u_all, x1t, vt_all, l, r2, e2, cnt, e1, x1, ln2_g[l], ln2_b[l], alpha,
                   tt=512, tn=512)
    return xf.reshape(batch, seq, d)
```
